```python
import math
import jax, jax.numpy as jnp
from jax import lax
import numpy as np

D_MODEL = 2048
BATCH = 2
SEQ = 8192
DEPTH = 4

N_MEM = 256
N_MIXERS = 3
MIX_WIDTH = 1536
XA_HEADS = 4
XA_HEAD_DIM = 128
XA_WIDTH = XA_HEADS * XA_HEAD_DIM
CAT_WIDTH = MIX_WIDTH + XA_WIDTH
HEAD_DIM = 64
N_Q_HEADS = MIX_WIDTH // HEAD_DIM
N_KV_HEADS = 3
Q_PER_KV = N_Q_HEADS // N_KV_HEADS
WINDOW = 128
ROPE_DIM = HEAD_DIM // 4
ROPE_THETA = 500000.0
SSD_HEAD_DIM = 64
SSD_HEADS = MIX_WIDTH // SSD_HEAD_DIM
SSD_GROUPS = 4
SSD_HEADS_PER_GROUP = SSD_HEADS // SSD_GROUPS
SSD_STATE = 128
SSD_CONV = 4
SSD_CHUNK = 128
SSD_CONV_CH = MIX_WIDTH + 2 * SSD_GROUPS * SSD_STATE
CONF_WIDTH = 31
N_EXPERTS = 32
TOP_K = 4
D_FF = 1024
SWIGLU_LIMIT = 7.0
SWIGLU_ALPHA = 1.702
MOE_BLOCK = 512
DN_ALPHA = (2 * DEPTH) ** 0.25
DN_BETA = (8 * DEPTH) ** -0.25
LN_EPS = 1e-5

ATTN_IN = MIX_WIDTH + 2 * N_KV_HEADS * HEAD_DIM + XA_WIDTH
SSD_IN = MIX_WIDTH + SSD_CONV_CH + SSD_HEADS + XA_WIDTH
CONF_IN = 2 * MIX_WIDTH + XA_WIDTH
N_ATTN = (DEPTH + 2) // 3
N_SSD = (DEPTH + 1) // 3
N_CONF = DEPTH // 3

kernel_name = "hybrid_swa_ssd_conformer_moe_deepnorm"


def layer_norm(x, g, b):
    xf = x.astype(jnp.float32)
    mu = jnp.mean(xf, -1, keepdims=True)
    var = jnp.mean(jnp.square(xf - mu), -1, keepdims=True)
    return ((xf - mu) * lax.rsqrt(var + LN_EPS)).astype(x.dtype) * g + b


def rms_norm(x, g):
    xf = x.astype(jnp.float32)
    return (xf * lax.rsqrt(jnp.mean(xf * xf, -1, keepdims=True) + LN_EPS)).astype(x.dtype) * g


def partial_rope(t, positions):
    half = ROPE_DIM // 2
    inv_freq = ROPE_THETA ** (-jnp.arange(half, dtype=jnp.float32) / half)
    ang = positions.astype(jnp.float32)[..., None] * inv_freq
    ang = ang[:, :, None, :]
    cos, sin = jnp.cos(ang).astype(t.dtype), jnp.sin(ang).astype(t.dtype)
    t1, t2, rest = t[..., :half], t[..., half:ROPE_DIM], t[..., ROPE_DIM:]
    return jnp.concatenate([t1 * cos - t2 * sin, t2 * cos + t1 * sin, rest], -1)


def causal_dwconv(x, w, b):
    K, C = w.shape
    y = lax.conv_general_dilated(x, w[:, None, :].astype(x.dtype), window_strides=(1,),
                                 padding=[(K - 1, 0)], dimension_numbers=('NWC', 'WIO', 'NWC'),
                                 feature_group_count=C)
    return y + b


def sliding_window_attention(q, k, v, sinks):
    Bsz, L = q.shape[:2]
    nb = L // WINDOW
    qb = q.reshape(Bsz, nb, WINDOW, N_KV_HEADS, Q_PER_KV, HEAD_DIM)

    def band(t):
        tb = t.reshape(Bsz, nb, WINDOW, N_KV_HEADS, HEAD_DIM)
        prev = jnp.pad(tb[:, :-1], ((0, 0), (1, 0), (0, 0), (0, 0), (0, 0)))
        return jnp.concatenate([prev, tb], axis=2)

    kb, vb = band(k), band(v)
    s = jnp.einsum('bnqgrd,bnsgd->bngrqs', qb, kb).astype(jnp.float32) * (HEAD_DIM ** -0.5)
    qi = jnp.arange(WINDOW)[:, None]
    sj = jnp.arange(2 * WINDOW)[None, :]
    rel = qi + WINDOW - sj
    band_mask = (rel >= 0) & (rel < WINDOW)
    mask = band_mask[None] & ((jnp.arange(nb)[:, None, None] > 0) | (sj[None] >= WINDOW))
    s = jnp.where(mask[None, :, None, None], s, -jnp.inf)
    sink = jnp.broadcast_to(sinks.astype(jnp.float32).reshape(1, 1, N_KV_HEADS, Q_PER_KV, 1, 1),
                            s.shape[:-1] + (1,))
    p = jax.nn.softmax(jnp.concatenate([s, sink], -1), axis=-1)[..., :-1]
    o = jnp.einsum('bngrqs,bnsgd->bnqgrd', p.astype(vb.dtype), vb)
    return o.reshape(Bsz, L, MIX_WIDTH)


def swa_mixer(u, positions, sinks):
    Bsz, L = u.shape[:2]
    q = u[..., :MIX_WIDTH].reshape(Bsz, L, N_Q_HEADS, HEAD_DIM)
    kv = u[..., MIX_WIDTH:].reshape(Bsz, L, 2, N_KV_HEADS, HEAD_DIM)
    k, v = kv[:, :, 0], kv[:, :, 1]
    q = partial_rope(q, positions).reshape(Bsz, L, N_KV_HEADS, Q_PER_KV, HEAD_DIM)
    k = partial_rope(k, positions)
    return sliding_window_attention(q, k, v, sinks)


def ssd_chunked_scan(x, dt, A, Bm, Cm):
    Bsz, L = x.shape[:2]
    nc, Q, G, R = L // SSD_CHUNK, SSD_CHUNK, SSD_GROUPS, SSD_HEADS_PER_GROUP
    xf = x.astype(jnp.float32).reshape(Bsz, nc, Q, G, R, SSD_HEAD_DIM)
    dtc = dt.reshape(Bsz, nc, Q, G, R)
    Bc = Bm.astype(jnp.float32).reshape(Bsz, nc, Q, G, SSD_STATE)
    Cc = Cm.astype(jnp.float32).reshape(Bsz, nc, Q, G, SSD_STATE)
    a_cs = jnp.cumsum(dtc * A.reshape(G, R), axis=2)
    xdt = xf * dtc[..., None]
    seg = a_cs[:, :, :, None] - a_cs[:, :, None, :]
    causal = jnp.tril(jnp.ones((Q, Q), bool))[None, None, :, :, None, None]
    decay = jnp.exp(jnp.where(causal, seg, -jnp.inf))
    cb = jnp.einsum('bclgn,bcsgn->bclsg', Cc, Bc)
    y_diag = jnp.einsum('bclsg,bclsgr,bcsgrp->bclgrp', cb, decay, xdt)
    decay_to_end = jnp.exp(a_cs[:, :, -1:] - a_cs)
    states = jnp.einsum('bclgn,bclgr,bclgrp->bcgrpn', Bc, decay_to_end, xdt)
    chunk_decay = jnp.exp(a_cs[:, :, -1])

    def step(h, inp):
        s_c, d_c = inp
        return h * d_c[..., None, None] + s_c, h

    h0 = jnp.zeros((Bsz, G, R, SSD_HEAD_DIM, SSD_STATE), jnp.float32)
    _, prev = lax.scan(step, h0, (jnp.moveaxis(states, 1, 0), jnp.moveaxis(chunk_decay, 1, 0)))
    prev = jnp.moveaxis(prev, 0, 1)
    y_off = jnp.einsum('bclgn,bcgrpn,bclgr->bclgrp', Cc, prev, jnp.exp(a_cs))
    return (y_diag + y_off).reshape(Bsz, L, SSD_HEADS, SSD_HEAD_DIM)


def ssd_mixer(u, conv_w, conv_b, dt_bias, a_log, d_skip, norm_g):
    Bsz, L = u.shape[:2]
    gn = SSD_GROUPS * SSD_STATE
    z = u[..., :MIX_WIDTH]
    xbc = jax.nn.silu(causal_dwconv(u[..., MIX_WIDTH:MIX_WIDTH + SSD_CONV_CH], conv_w, conv_b))
    dt_raw = u[..., MIX_WIDTH + SSD_CONV_CH:]
    xs = xbc[..., :MIX_WIDTH].reshape(Bsz, L, SSD_HEADS, SSD_HEAD_DIM)
    Bm = xbc[..., MIX_WIDTH:MIX_WIDTH + gn].reshape(Bsz, L, SSD_GROUPS, SSD_STATE)
    Cm = xbc[..., MIX_WIDTH + gn:].reshape(Bsz, L, SSD_GROUPS, SSD_STATE)
    dt = jax.nn.softplus(dt_raw.astype(jnp.float32) + dt_bias.astype(jnp.float32))
    A = -jnp.exp(a_log.astype(jnp.float32))
    y = ssd_chunked_scan(xs, dt, A, Bm, Cm)
    y = y + d_skip.astype(jnp.float32)[:, None] * xs.astype(jnp.float32)
    y = y.reshape(Bsz, L, MIX_WIDTH).astype(u.dtype)
    return rms_norm(y * jax.nn.silu(z), norm_g)


def conformer_conv(u, dw_w, dw_b, ln_g, ln_b):
    h = u[..., :MIX_WIDTH] * jax.nn.sigmoid(u[..., MIX_WIDTH:])
    h = causal_dwconv(h, dw_w, dw_b)
    h = layer_norm(h, ln_g, ln_b)
    return jax.nn.silu(h)


def memory_attention(q, mem, w_kv):
    Bsz, L = q.shape[:2]
    qh = q.reshape(Bsz, L, XA_HEADS, XA_HEAD_DIM)
    kv = (mem @ w_kv).reshape(Bsz, mem.shape[1], 2, XA_HEADS, XA_HEAD_DIM)
    mk, mv = kv[:, :, 0], kv[:, :, 1]
    s = jnp.einsum('blhd,bmhd->bhlm', qh, mk).astype(jnp.float32) * (XA_HEAD_DIM ** -0.5)
    p = jax.nn.softmax(s, axis=-1).astype(mv.dtype)
    return jnp.einsum('bhlm,bmhd->blhd', p, mv).reshape(Bsz, L, XA_WIDTH)


def moe_ffn(x, router_w, router_b, w1, b1, w2, b2):
    Bsz, L, D = x.shape
    T = Bsz * L
    xt = x.reshape(T, D)
    logits = (xt @ router_w).astype(jnp.float32) + router_b.astype(jnp.float32)
    top_val, top_idx = lax.top_k(logits, TOP_K)
    gates = jax.nn.softmax(top_val, axis=-1)
    flat_e = top_idx.reshape(-1)
    flat_t = jnp.arange(T * TOP_K, dtype=jnp.int32) // TOP_K
    flat_g = gates.reshape(-1)
    order = jnp.argsort(flat_e)
    se = flat_e[order]
    counts = jnp.bincount(flat_e, length=N_EXPERTS)
    start = jnp.cumsum(counts) - counts
    pcounts = (counts + MOE_BLOCK - 1) // MOE_BLOCK * MOE_BLOCK
    pend = jnp.cumsum(pcounts)
    pstart = pend - pcounts
    dest = pstart[se] + (jnp.arange(T * TOP_K) - start[se])
    n_blocks = -(-(T * TOP_K) // MOE_BLOCK) + N_EXPERTS
    S = n_blocks * MOE_BLOCK
    buf_tok = jnp.full((S,), T, jnp.int32).at[dest].set(flat_t[order])
    buf_g = jnp.zeros((S,), x.dtype).at[dest].set(flat_g[order].astype(x.dtype))
    block_e = jnp.minimum(jnp.searchsorted(pend, jnp.arange(n_blocks) * MOE_BLOCK, side='right'),
                          N_EXPERTS - 1)
    x_pad = jnp.concatenate([xt, jnp.zeros((1, D), xt.dtype)], axis=0)

    def run_block(args):
        tok, g, e = args
        h = x_pad[tok] @ w1[e] + b1[e]
        glu = jnp.minimum(h[:, :D_FF], SWIGLU_LIMIT)
        lin = jnp.clip(h[:, D_FF:], -SWIGLU_LIMIT, SWIGLU_LIMIT)
        act = glu * jax.nn.sigmoid(SWIGLU_ALPHA * glu) * (lin + 1)
        return (act @ w2[e] + b2[e]) * g[:, None]

    out = lax.map(run_block, (buf_tok.reshape(n_blocks, MOE_BLOCK),
                              buf_g.reshape(n_blocks, MOE_BLOCK), block_e))
    y = jax.ops.segment_sum(out.reshape(S, D), buf_tok, num_segments=T + 1)[:T]
    return y.reshape(Bsz, L, D)


def setup_inputs(seed: int = 0) -> dict:
    key = jax.random.key(seed)
    ks = iter(jax.random.split(key, 64))

    def nrm(shape, scale):
        return jax.random.normal(next(ks), shape, jnp.float32) * scale

    D = D_MODEL
    x = nrm((BATCH, SEQ, D), 1.0)
    mem = nrm((BATCH, N_MEM, D), 1.0)
    offs = jax.random.randint(next(ks), (BATCH, 1), 0, 4096, jnp.int32)
    positions = offs + jnp.arange(SEQ, dtype=jnp.int32)[None, :]
    dt0 = jnp.exp(jax.random.uniform(next(ks), (N_SSD, SSD_HEADS), jnp.float32)
                  * (math.log(0.1) - math.log(0.001)) + math.log(0.001))
    ssd_dt_bias = dt0 + jnp.log(-jnp.expm1(-dt0))
    ssd_a_log = jnp.log(jax.random.uniform(next(ks), (N_SSD, SSD_HEADS), jnp.float32, 1.0, 16.0))
    return {
        "x": x, "mem": mem, "positions": positions,
        "attn_w_in": nrm((N_ATTN, D, ATTN_IN), D ** -0.5),
        "attn_b_in": nrm((N_ATTN, ATTN_IN), 0.02),
        "attn_sinks": nrm((N_ATTN, N_Q_HEADS), 0.5),
        "ssd_w_in": nrm((N_SSD, D, SSD_IN), D ** -0.5),
        "ssd_b_in": nrm((N_SSD, SSD_IN), 0.02),
        "ssd_conv_w": nrm((N_SSD, SSD_CONV, SSD_CONV_CH), SSD_CONV ** -0.5),
        "ssd_conv_b": nrm((N_SSD, SSD_CONV_CH), 0.02),
        "ssd_dt_bias": ssd_dt_bias,
        "ssd_a_log": ssd_a_log,
        "ssd_d_skip": 1.0 + nrm((N_SSD, SSD_HEADS), 0.1),
        "ssd_norm_g": 1.0 + nrm((N_SSD, MIX_WIDTH), 0.02),
        "conf_w_in": nrm((N_CONF, D, CONF_IN), D ** -0.5),
        "conf_b_in": nrm((N_CONF, CONF_IN), 0.02),
        "conf_dw_w": nrm((N_CONF, CONF_WIDTH, MIX_WIDTH), CONF_WIDTH ** -0.5),
        "conf_dw_b": nrm((N_CONF, MIX_WIDTH), 0.02),
        "conf_ln_g": 1.0 + nrm((N_CONF, MIX_WIDTH), 0.02),
        "conf_ln_b": nrm((N_CONF, MIX_WIDTH), 0.02),
        "mem_w_kv": nrm((DEPTH, D, 2 * XA_WIDTH), D ** -0.5),
        "w_out": nrm((DEPTH, CAT_WIDTH, D), CAT_WIDTH ** -0.5 * DN_BETA),
        "b_out": nrm((DEPTH, D), 0.02),
        "ln1_g": 1.0 + nrm((DEPTH, D), 0.02),
        "ln1_b": nrm((DEPTH, D), 0.02),
        "router_w": nrm((DEPTH, D, N_EXPERTS), D ** -0.5),
        "router_b": nrm((DEPTH, N_EXPERTS), 0.01),
        "moe_w1": nrm((DEPTH, N_EXPERTS, D, 2 * D_FF), D ** -0.5),
        "moe_b1": nrm((DEPTH, N_EXPERTS, 2 * D_FF), 0.02),
        "moe_w2": nrm((DEPTH, N_EXPERTS, D_FF, D), D_FF ** -0.5 * DN_BETA),
        "moe_b2": nrm((DEPTH, N_EXPERTS, D), 0.02),
        "ln2_g": 1.0 + nrm((DEPTH, D), 0.02),
        "ln2_b": nrm((DEPTH, D), 0.02),
    }


def reference(x, mem, positions,
              attn_w_in, attn_b_in, attn_sinks,
              ssd_w_in, ssd_b_in, ssd_conv_w, ssd_conv_b, ssd_dt_bias, ssd_a_log, ssd_d_skip, ssd_norm_g,
              conf_w_in, conf_b_in, conf_dw_w, conf_dw_b, conf_ln_g, conf_ln_b,
              mem_w_kv, w_out, b_out, ln1_g, ln1_b,
              router_w, router_b, moe_w1, moe_b1, moe_w2, moe_b2, ln2_g, ln2_b):
    h = x
    for i in range(DEPTH):
        kind, j = i % N_MIXERS, i // N_MIXERS
        if kind == 0:
            u = h @ attn_w_in[j] + attn_b_in[j]
            mix = swa_mixer(u[..., :-XA_WIDTH], positions, attn_sinks[j])
        elif kind == 1:
            u = h @ ssd_w_in[j] + ssd_b_in[j]
            mix = ssd_mixer(u[..., :-XA_WIDTH], ssd_conv_w[j], ssd_conv_b[j], ssd_dt_bias[j],
                            ssd_a_log[j], ssd_d_skip[j], ssd_norm_g[j])
        else:
            u = h @ conf_w_in[j] + conf_b_in[j]
            mix = conformer_conv(u[..., :-XA_WIDTH], conf_dw_w[j], conf_dw_b[j], conf_ln_g[j], conf_ln_b[j])
        xa = memory_attention(u[..., -XA_WIDTH:], mem, mem_w_kv[i])
        sub = jnp.concatenate([mix.astype(h.dtype), xa.astype(h.dtype)], axis=-1) @ w_out[i] + b_out[i]
        h = layer_norm(DN_ALPHA * h + sub, ln1_g[i], ln1_b[i])
        ffn = moe_ffn(h, router_w[i], router_b[i], moe_w1[i], moe_b1[i], moe_w2[i], moe_b2[i])
        h = layer_norm(DN_ALPHA * h + ffn, ln2_g[i], ln2_b[i])
    return h
```

```python
import functools
import math

import jax
import jax.numpy as jnp
from jax import lax
from jax.experimental import pallas as pl
from jax.experimental.pallas import tpu as pltpu

F32 = jnp.float32
BF16 = jnp.bfloat16
U32 = jnp.uint32
I32 = jnp.int32

LANES = 128
PACK_SLABS = 8
PACK_SPAN = 2 * LANES

D_MODEL = 2048
N_MIXERS = 3
MIX_WIDTH = 1536
XA_HEADS = 4
XA_HEAD_DIM = 128
XA_WIDTH = XA_HEADS * XA_HEAD_DIM
HEAD_DIM = 64
N_Q_HEADS = MIX_WIDTH // HEAD_DIM
N_KV_HEADS = 3
Q_PER_KV = N_Q_HEADS // N_KV_HEADS
WINDOW = 128
ROPE_DIM = HEAD_DIM // 4
ROPE_HALF = ROPE_DIM // 2
ROPE_THETA = 500000.0
SSD_HEAD_DIM = 64
SSD_HEADS = MIX_WIDTH // SSD_HEAD_DIM
SSD_GROUPS = 4
SSD_HPG = SSD_HEADS // SSD_GROUPS
SSD_GROUP_WIDTH = SSD_HPG * SSD_HEAD_DIM
SSD_STATE = 128
SSD_CONV = 4
SSD_CHUNK = 128
SSD_BC = SSD_GROUPS * SSD_STATE
SSD_CONV_CH = MIX_WIDTH + 2 * SSD_BC
CONF_WIDTH = 31
CONF_HALO = 32
SSD_HALO = 8
TOP_K = 4
D_FF = 1024
SWIGLU_LIMIT = 7.0
SWIGLU_ALPHA = 1.702
LN_EPS = 1e-5

MOE_BM = 512
MOE_TF = 512
ROUTER_TM = 256
COMBINE_TM = 256
DISPATCH_TM = 512
CONF_TR = 256
XA_TQ = 512
OUT_TM = 512

HI_MASK = 0xFFFF0000


def _sigmoid(x):
    return 1.0 / (1.0 + jnp.exp(-x))


def _silu(x):
    return x * _sigmoid(x)


def _pick_tile(n, cap):
    best = LANES
    for t in range(LANES, cap + 1, LANES):
        if n % t == 0:
            best = t
    return best


def _pack_words(a, b):
    a = a.astype(BF16).astype(F32)
    b = b.astype(BF16).astype(F32)
    return (pltpu.bitcast(a, U32) >> 16) | (pltpu.bitcast(b, U32) & jnp.uint32(HI_MASK))


def _unpack_words(w):
    lo = pltpu.bitcast(w << 16, F32)
    hi = pltpu.bitcast(w & jnp.uint32(HI_MASK), F32)
    return lo, hi


def _layer_norm_rows(y, g, b):
    mu = jnp.mean(y, axis=-1, keepdims=True)
    d = y - mu
    var = jnp.mean(d * d, axis=-1, keepdims=True)
    return d * lax.rsqrt(var + LN_EPS) * g + b


def _linear_kernel(x_ref, w_ref, b_ref, o_ref):
    x = x_ref[...].astype(BF16)
    acc = jnp.dot(x, w_ref[...], preferred_element_type=F32) + b_ref[...]
    o_ref[...] = acc.astype(o_ref.dtype)


def _linear(x, w, b, out_dtype, tm=512, tn_cap=1024):
    m, k = x.shape
    n = w.shape[1]
    tn = _pick_tile(n, tn_cap)
    tm = min(tm, m)
    return pl.pallas_call(
        _linear_kernel,
        grid=(n // tn, m // tm),
        in_specs=[
            pl.BlockSpec((tm, k), lambda j, i: (i, 0)),
            pl.BlockSpec((k, tn), lambda j, i: (0, j)),
            pl.BlockSpec((1, tn), lambda j, i: (0, j)),
        ],
        out_specs=pl.BlockSpec((tm, tn), lambda j, i: (i, j)),
        out_shape=jax.ShapeDtypeStruct((m, n), out_dtype),
        compiler_params=pltpu.CompilerParams(dimension_semantics=("arbitrary", "arbitrary")),
        name="linear",
    )(x, w.astype(BF16), b.reshape(1, n).astype(F32))


def _outproj_ln_kernel(alpha, mix_ref, xa_ref, wm_ref, wx_ref, b_ref, h_ref, g_ref, be_ref,
                       o_ref, obf_ref, opk_ref):
    sub = jnp.dot(mix_ref[...], wm_ref[...], preferred_element_type=F32)
    sub = sub + jnp.dot(xa_ref[...], wx_ref[...], preferred_element_type=F32) + b_ref[...]
    y = _layer_norm_rows(alpha * h_ref[...] + sub, g_ref[...], be_ref[...])
    o_ref[...] = y
    obf_ref[...] = y.astype(BF16)
    tm = y.shape[0]
    for s in range(PACK_SLABS):
        lo = y[:, PACK_SPAN * s:PACK_SPAN * s + LANES]
        hi = y[:, PACK_SPAN * s + LANES:PACK_SPAN * (s + 1)]
        opk_ref[pl.ds(s, tm, stride=PACK_SLABS), :] = _pack_words(lo, hi)


def _outproj_ln(mix, xa, w_out, b_out, h, g, be, alpha):
    t, d = h.shape
    tm = min(OUT_TM, t)
    wm = w_out[:MIX_WIDTH].astype(BF16)
    wx = w_out[MIX_WIDTH:].astype(BF16)
    row = lambda i: (i, 0)
    fixed = lambda i: (0, 0)
    return pl.pallas_call(
        functools.partial(_outproj_ln_kernel, alpha),
        grid=(t // tm,),
        in_specs=[
            pl.BlockSpec((tm, MIX_WIDTH), row),
            pl.BlockSpec((tm, XA_WIDTH), row),
            pl.BlockSpec((MIX_WIDTH, d), fixed),
            pl.BlockSpec((XA_WIDTH, d), fixed),
            pl.BlockSpec((1, d), fixed),
            pl.BlockSpec((tm, d), row),
            pl.BlockSpec((1, d), fixed),
            pl.BlockSpec((1, d), fixed),
        ],
        out_specs=[
            pl.BlockSpec((tm, d), row),
            pl.BlockSpec((tm, d), row),
            pl.BlockSpec((tm * PACK_SLABS, LANES), row),
        ],
        out_shape=[
            jax.ShapeDtypeStruct((t, d), F32),
            jax.ShapeDtypeStruct((t, d), BF16),
            jax.ShapeDtypeStruct((t * PACK_SLABS, LANES), U32),
        ],
        compiler_params=pltpu.CompilerParams(dimension_semantics=("arbitrary",)),
        name="outproj_ln",
    )(mix, xa, wm, wx, b_out.reshape(1, d), h, g.reshape(1, d), be.reshape(1, d))


def _xa_kernel(q_ref, k_ref, v_ref, o_ref):
    scale = XA_HEAD_DIM ** -0.5
    outs = []
    for hd in range(XA_HEADS):
        sl = slice(hd * XA_HEAD_DIM, (hd + 1) * XA_HEAD_DIM)
        s = lax.dot_general(q_ref[:, sl], k_ref[:, sl], (((1,), (1,)), ((), ())),
                            preferred_element_type=F32) * scale
        m = jnp.max(s, axis=-1, keepdims=True)
        p = jnp.exp(s - m)
        den = jnp.sum(p, axis=-1, keepdims=True)
        o = jnp.dot(p.astype(BF16), v_ref[:, sl], preferred_element_type=F32)
        outs.append(o / den)
    o_ref[...] = jnp.concatenate(outs, axis=-1).astype(o_ref.dtype)


def _memory_attention(qm, kv, bsz, seq):
    n_mem = kv.shape[0] // bsz
    tq = min(XA_TQ, seq)
    nq = seq // tq
    return pl.pallas_call(
        _xa_kernel,
        grid=(bsz, nq),
        in_specs=[
            pl.BlockSpec((tq, XA_WIDTH), lambda b, i: (b * nq + i, 0)),
            pl.BlockSpec((n_mem, XA_WIDTH), lambda b, i: (b, 0)),
            pl.BlockSpec((n_mem, XA_WIDTH), lambda b, i: (b, 1)),
        ],
        out_specs=pl.BlockSpec((tq, XA_WIDTH), lambda b, i: (b * nq + i, 0)),
        out_shape=jax.ShapeDtypeStruct((bsz * seq, XA_WIDTH), BF16),
        compiler_params=pltpu.CompilerParams(dimension_semantics=("arbitrary", "arbitrary")),
        name="memory_attention",
    )(qm, kv, kv)


def _rope_tables(pos, freq_row):
    ang = pos * freq_row
    c = jnp.cos(ang)
    s = jnp.sin(ang)
    d = lax.broadcasted_iota(I32, ang.shape, 1) % HEAD_DIM
    cmul = jnp.where(d < ROPE_DIM, c, 1.0)
    s_up = jnp.where(d < ROPE_HALF, -s, 0.0)
    s_dn = jnp.where((d >= ROPE_HALF) & (d < ROPE_DIM), s, 0.0)
    return cmul, s_up, s_dn


def _rope_chunk(t, tabs):
    cmul, s_up, s_dn = tabs
    up = pltpu.roll(t, LANES - ROPE_HALF, axis=1)
    dn = pltpu.roll(t, ROPE_HALF, axis=1)
    return t * cmul + up * s_up + dn * s_dn


def _swa_kernel(sinks_ref, q_ref, kvp_ref, kvc_ref, posp_ref, posc_ref, freq_ref, o_ref):
    n = pl.program_id(1)
    w = WINDOW
    freq = freq_ref[...]
    tab_c = _rope_tables(posc_ref[...], freq)
    tab_p = _rope_tables(posp_ref[...], freq)
    lane = lax.broadcasted_iota(I32, (w, LANES), 1)
    first_half = lane < HEAD_DIM

    def rope_kv(kv_ref, tabs):
        c0 = _rope_chunk(kv_ref[:, 0:LANES].astype(F32), tabs)
        c1 = kv_ref[:, LANES:2 * LANES].astype(F32)
        c1 = jnp.where(first_half, _rope_chunk(c1, tabs), c1)
        c2 = kv_ref[:, 2 * LANES:3 * LANES].astype(F32)
        return jnp.concatenate([c0, c1, c2], axis=1)

    kv_c = rope_kv(kvc_ref, tab_c)
    kv_p = rope_kv(kvp_ref, tab_p)
    kv = jnp.concatenate([kv_p, kv_c], axis=0).astype(BF16)

    scale = HEAD_DIM ** -0.5
    qs = []
    for j in range(MIX_WIDTH // LANES):
        qj = _rope_chunk(q_ref[:, j * LANES:(j + 1) * LANES].astype(F32), tab_c) * scale
        qs.append(qj.astype(BF16))

    rows = Q_PER_KV * w
    qi = lax.broadcasted_iota(I32, (rows, 2 * w), 0) % w
    sj = lax.broadcasted_iota(I32, (rows, 2 * w), 1)
    rel = qi + w - sj
    mask = (rel >= 0) & (rel < w) & ((n > 0) | (sj >= w))

    outs = []
    for g in range(N_KV_HEADS):
        k_g = kv[:, g * HEAD_DIM:(g + 1) * HEAD_DIM]
        v_g = kv[:, (N_KV_HEADS + g) * HEAD_DIM:(N_KV_HEADS + g + 1) * HEAD_DIM]
        q_parts = []
        sink_parts = []
        for r in range(Q_PER_KV):
            hd = g * Q_PER_KV + r
            chunk = qs[hd // 2]
            q_parts.append(chunk[:, (hd % 2) * HEAD_DIM:(hd % 2 + 1) * HEAD_DIM])
            sink_parts.append(jnp.full((w, 1), sinks_ref[hd], F32))
        q_g = jnp.concatenate(q_parts, axis=0)
        sink = jnp.concatenate(sink_parts, axis=0)
        s = lax.dot_general(q_g, k_g, (((1,), (1,)), ((), ())), preferred_element_type=F32)
        s = jnp.where(mask, s, -jnp.inf)
        m = jnp.maximum(jnp.max(s, axis=-1, keepdims=True), sink)
        p = jnp.exp(s - m)
        den = jnp.sum(p, axis=-1, keepdims=True) + jnp.exp(sink - m)
        o = jnp.dot(p.astype(BF16), v_g, preferred_element_type=F32) / den
        for r in range(Q_PER_KV):
            outs.append(o[r * w:(r + 1) * w, :])
    o_ref[...] = jnp.concatenate(outs, axis=1).astype(o_ref.dtype)


def _swa_mixer(u, posf, sinks, freq_row, bsz, seq):
    nb = seq // WINDOW
    kv_w = 2 * N_KV_HEADS * HEAD_DIM
    kv_col = MIX_WIDTH // kv_w
    cur = lambda b, n, s: (b * nb + n, 0)
    prev = lambda b, n, s: (b * nb + jnp.maximum(n - 1, 0), 0)
    return pl.pallas_call(
        _swa_kernel,
        grid_spec=pltpu.PrefetchScalarGridSpec(
            num_scalar_prefetch=1,
            grid=(bsz, nb),
            in_specs=[
                pl.BlockSpec((WINDOW, MIX_WIDTH), cur),
                pl.BlockSpec((WINDOW, kv_w), lambda b, n, s: (b * nb + jnp.maximum(n - 1, 0), kv_col)),
                pl.BlockSpec((WINDOW, kv_w), lambda b, n, s: (b * nb + n, kv_col)),
                pl.BlockSpec((WINDOW, 1), prev),
                pl.BlockSpec((WINDOW, 1), cur),
                pl.BlockSpec((1, LANES), lambda b, n, s: (0, 0)),
            ],
            out_specs=pl.BlockSpec((WINDOW, MIX_WIDTH), cur),
        ),
        out_shape=jax.ShapeDtypeStruct((bsz * seq, MIX_WIDTH), BF16),
        compiler_params=pltpu.CompilerParams(dimension_semantics=("arbitrary", "arbitrary")),
        name="swa",
    )(sinks.astype(F32), u, u, u, posf, posf, freq_row)


def _split3(x):
    p0 = x.astype(BF16)
    r = x - p0.astype(F32)
    p1 = r.astype(BF16)
    p2 = (r - p1.astype(F32)).astype(BF16)
    return p0, p1, p2


def _expand_heads(arr, g):
    rows = arr.shape[0]
    parts = [jnp.broadcast_to(arr[:, g * SSD_HPG + r:g * SSD_HPG + r + 1], (rows, SSD_HEAD_DIM))
             for r in range(SSD_HPG)]
    return jnp.concatenate(parts, axis=1)


def _ssd_kernel(z_ref, xbc_ref, dt_ref, cw_ref, cb_ref, dtb_ref, alog_ref, dskip_ref, ng_ref,
                o_ref, ext_ref, state_ref):
    c = pl.program_id(1)
    q = SSD_CHUNK

    @pl.when(c == 0)
    def _():
        ext_ref[0:SSD_HALO, :] = jnp.zeros((SSD_HALO, SSD_CONV_CH), F32)
        state_ref[...] = jnp.zeros(state_ref.shape, F32)

    ext_ref[SSD_HALO:SSD_HALO + q, :] = xbc_ref[...].astype(F32)
    conv = cb_ref[...]
    for k in range(SSD_CONV):
        conv = conv + cw_ref[k:k + 1, :] * ext_ref[pl.ds(SSD_HALO - (SSD_CONV - 1) + k, q), :]
    ext_ref[0:SSD_HALO, :] = ext_ref[q:q + SSD_HALO, :]
    xbc = _silu(conv)

    lane = lax.broadcasted_iota(I32, (q, LANES), 1)
    x = dt_ref[...] + dtb_ref[...]
    dt = jnp.maximum(x, 0.0) + jnp.log1p(jnp.exp(-jnp.abs(x)))
    a = jnp.where(lane < SSD_HEADS, dt * (-jnp.exp(alog_ref[...])), 0.0)

    row = lax.broadcasted_iota(I32, (q, q), 0)
    col = lax.broadcasted_iota(I32, (q, q), 1)
    causal = row >= col
    tri = causal.astype(BF16)
    a_cs = jnp.zeros((q, LANES), F32)
    for piece in _split3(a):
        a_cs = a_cs + jnp.dot(tri, piece, preferred_element_type=F32)
    a_cs_t = a_cs.T
    a_last = jnp.broadcast_to(a_cs[q - 1:q, :], (q, LANES))

    ys = []
    for g in range(SSD_GROUPS):
        xs_g = xbc[:, g * SSD_GROUP_WIDTH:(g + 1) * SSD_GROUP_WIDTH]
        b_g = xbc[:, MIX_WIDTH + g * SSD_STATE:MIX_WIDTH + (g + 1) * SSD_STATE].astype(BF16)
        c_g = xbc[:, MIX_WIDTH + SSD_BC + g * SSD_STATE:MIX_WIDTH + SSD_BC + (g + 1) * SSD_STATE].astype(BF16)
        cb = lax.dot_general(c_g, b_g, (((1,), (1,)), ((), ())), preferred_element_type=F32)
        acs_e = _expand_heads(a_cs, g)
        last_e = _expand_heads(a_last, g)
        xdt = xs_g * _expand_heads(dt, g)
        xdt_bf = xdt.astype(BF16)

        diag = []
        for r in range(SSD_HPG):
            hd = g * SSD_HPG + r
            seg = a_cs[:, hd:hd + 1] - a_cs_t[hd:hd + 1, :]
            decay = jnp.exp(jnp.where(causal, seg, -jnp.inf))
            mat = (cb * decay).astype(BF16)
            diag.append(jnp.dot(mat, xdt_bf[:, r * SSD_HEAD_DIM:(r + 1) * SSD_HEAD_DIM],
                                preferred_element_type=F32))
        y_diag = jnp.concatenate(diag, axis=1)

        prev = state_ref[g]
        y_off = jnp.dot(c_g, prev.astype(BF16), preferred_element_type=F32) * jnp.exp(acs_e)
        wgt = (xdt * jnp.exp(last_e - acs_e)).astype(BF16)
        upd = lax.dot_general(b_g, wgt, (((0,), (0,)), ((), ())), preferred_element_type=F32)
        state_ref[g] = prev * jnp.exp(last_e[0:1, :]) + upd
        ys.append(y_diag + y_off + dskip_ref[:, g * SSD_GROUP_WIDTH:(g + 1) * SSD_GROUP_WIDTH] * xs_g)

    y = jnp.concatenate(ys, axis=1) * _silu(z_ref[...].astype(F32))
    ms = jnp.mean(y * y, axis=-1, keepdims=True)
    o_ref[...] = (y * lax.rsqrt(ms + LN_EPS) * ng_ref[...]).astype(o_ref.dtype)


def _ssd_mixer(z, xbc, dt_raw, conv_w, conv_b, dt_bias, a_log, d_skip, norm_g, bsz, seq):
    nc = seq // SSD_CHUNK
    pad = LANES - SSD_HEADS
    row = lambda b, c: (b * nc + c, 0)
    fixed = lambda b, c: (0, 0)
    return pl.pallas_call(
        _ssd_kernel,
        grid=(bsz, nc),
        in_specs=[
            pl.BlockSpec((SSD_CHUNK, MIX_WIDTH), row),
            pl.BlockSpec((SSD_CHUNK, SSD_CONV_CH), row),
            pl.BlockSpec((SSD_CHUNK, LANES), row),
            pl.BlockSpec((SSD_CONV, SSD_CONV_CH), fixed),
            pl.BlockSpec((1, SSD_CONV_CH), fixed),
            pl.BlockSpec((1, LANES), fixed),
            pl.BlockSpec((1, LANES), fixed),
            pl.BlockSpec((1, MIX_WIDTH), fixed),
            pl.BlockSpec((1, MIX_WIDTH), fixed),
        ],
        out_specs=pl.BlockSpec((SSD_CHUNK, MIX_WIDTH), row),
        out_shape=jax.ShapeDtypeStruct((bsz * seq, MIX_WIDTH), BF16),
        scratch_shapes=[
            pltpu.VMEM((SSD_HALO + SSD_CHUNK, SSD_CONV_CH), F32),
            pltpu.VMEM((SSD_GROUPS, SSD_STATE, SSD_GROUP_WIDTH), F32),
        ],
        compiler_params=pltpu.CompilerParams(dimension_semantics=("arbitrary", "arbitrary")),
        name="ssd",
    )(z, xbc, dt_raw, conv_w, conv_b.reshape(1, -1),
      jnp.pad(dt_bias, (0, pad)).reshape(1, LANES), jnp.pad(a_log, (0, pad)).reshape(1, LANES),
      jnp.repeat(d_skip, SSD_HEAD_DIM).reshape(1, MIX_WIDTH), norm_g.reshape(1, MIX_WIDTH))


def _conformer_kernel(a_ref, b_ref, w_ref, wb_ref, g_ref, be_ref, o_ref, ext_ref, acc_ref):
    i = pl.program_id(1)
    tr = a_ref.shape[0]

    @pl.when(i == 0)
    def _():
        ext_ref[0:CONF_HALO, :] = jnp.zeros((CONF_HALO, MIX_WIDTH), F32)

    ext_ref[CONF_HALO:CONF_HALO + tr, :] = a_ref[...].astype(F32) * _sigmoid(b_ref[...].astype(F32))
    first = CONF_HALO - (CONF_WIDTH - 1)
    for j in range(MIX_WIDTH // LANES):
        cs = slice(j * LANES, (j + 1) * LANES)
        acc = jnp.broadcast_to(wb_ref[:, cs], (tr, LANES))
        for k in range(CONF_WIDTH):
            acc = acc + w_ref[k:k + 1, cs] * ext_ref[pl.ds(first + k, tr), cs]
        acc_ref[:, cs] = acc
    ext_ref[0:CONF_HALO, :] = ext_ref[tr:tr + CONF_HALO, :]
    y = _layer_norm_rows(acc_ref[...], g_ref[...], be_ref[...])
    o_ref[...] = _silu(y).astype(o_ref.dtype)


def _conformer_mixer(u, dw_w, dw_b, ln_g, ln_b, bsz, seq):
    tr = min(CONF_TR, seq)
    nr = seq // tr
    fixed = lambda b, i: (0, 0)
    return pl.pallas_call(
        _conformer_kernel,
        grid=(bsz, nr),
        in_specs=[
            pl.BlockSpec((tr, MIX_WIDTH), lambda b, i: (b * nr + i, 0)),
            pl.BlockSpec((tr, MIX_WIDTH), lambda b, i: (b * nr + i, 1)),
            pl.BlockSpec((CONF_WIDTH, MIX_WIDTH), fixed),
            pl.BlockSpec((1, MIX_WIDTH), fixed),
            pl.BlockSpec((1, MIX_WIDTH), fixed),
            pl.BlockSpec((1, MIX_WIDTH), fixed),
        ],
        out_specs=pl.BlockSpec((tr, MIX_WIDTH), lambda b, i: (b * nr + i, 0)),
        out_shape=jax.ShapeDtypeStruct((bsz * seq, MIX_WIDTH), BF16),
        scratch_shapes=[
            pltpu.VMEM((CONF_HALO + tr, MIX_WIDTH), F32),
            pltpu.VMEM((tr, MIX_WIDTH), F32),
        ],
        compiler_params=pltpu.CompilerParams(dimension_semantics=("arbitrary", "arbitrary")),
        name="conformer",
    )(u, u, dw_w, dw_b.reshape(1, -1), ln_g.reshape(1, -1), ln_b.reshape(1, -1))


def _router_kernel(n_experts, h_ref, w_ref, b_ref, idx_ref, gate_ref, rank_ref, cnt_ref, run_ref):
    i = pl.program_id(0)
    tm = h_ref.shape[0]

    @pl.when(i == 0)
    def _():
        run_ref[...] = jnp.zeros(run_ref.shape, F32)

    logits = jnp.dot(h_ref[...], w_ref[...], precision=lax.Precision.HIGHEST,
                     preferred_element_type=F32) + b_ref[...]
    lane = lax.broadcasted_iota(I32, (tm, LANES), 1).astype(F32)
    cur = jnp.where(lane < n_experts, logits, -jnp.inf)
    vals, idxs, hots = [], [], []
    for _ in range(TOP_K):
        m = jnp.max(cur, axis=-1, keepdims=True)
        ix = jnp.min(jnp.where(cur == m, lane, float(LANES)), axis=-1, keepdims=True)
        hot = lane == ix
        vals.append(m)
        idxs.append(ix)
        hots.append(hot)
        cur = jnp.where(hot, -jnp.inf, cur)
    exps = [jnp.exp(v - vals[0]) for v in vals]
    den = exps[0]
    for e in exps[1:]:
        den = den + e

    r = lax.broadcasted_iota(I32, (tm, tm), 0)
    c = lax.broadcasted_iota(I32, (tm, tm), 1)
    before = (r > c).astype(BF16)
    base = run_ref[0:1, :]
    ranks = []
    for hot in hots:
        hot_f = hot.astype(F32)
        earlier = jnp.dot(before, hot.astype(BF16), preferred_element_type=F32)
        ranks.append(jnp.sum(hot_f * (earlier + base), axis=-1, keepdims=True))
        base = base + jnp.sum(hot_f, axis=0, keepdims=True)
    run_ref[0:1, :] = base
    cnt_ref[...] = jnp.broadcast_to(base, cnt_ref.shape)

    idx_out = jnp.zeros((tm, LANES), F32)
    gate_out = jnp.zeros((tm, LANES), F32)
    rank_out = jnp.zeros((tm, LANES), F32)
    for k in range(TOP_K):
        sel = lane == k
        idx_out = jnp.where(sel, idxs[k], idx_out)
        gate_out = jnp.where(sel, exps[k] / den, gate_out)
        rank_out = jnp.where(sel, ranks[k], rank_out)
    idx_ref[...] = idx_out.astype(I32)
    gate_ref[...] = gate_out
    rank_ref[...] = rank_out


def _router(h, router_w, router_b):
    t, d = h.shape
    n_experts = router_w.shape[1]
    tm = min(ROUTER_TM, t)
    w = jnp.pad(router_w.astype(F32), ((0, 0), (0, LANES - n_experts)))
    b = jnp.pad(router_b.astype(F32), (0, LANES - n_experts)).reshape(1, LANES)
    row = lambda i: (i, 0)
    fixed = lambda i: (0, 0)
    return pl.pallas_call(
        functools.partial(_router_kernel, n_experts),
        grid=(t // tm,),
        in_specs=[pl.BlockSpec((tm, d), row), pl.BlockSpec((d, LANES), fixed), pl.BlockSpec((1, LANES), fixed)],
        out_specs=[pl.BlockSpec((tm, LANES), row), pl.BlockSpec((tm, LANES), row),
                   pl.BlockSpec((tm, LANES), row), pl.BlockSpec((8, LANES), fixed)],
        out_shape=[jax.ShapeDtypeStruct((t, LANES), I32), jax.ShapeDtypeStruct((t, LANES), F32),
                   jax.ShapeDtypeStruct((t, LANES), F32), jax.ShapeDtypeStruct((8, LANES), F32)],
        scratch_shapes=[pltpu.VMEM((8, LANES), F32)],
        compiler_params=pltpu.CompilerParams(dimension_semantics=("arbitrary",)),
        name="router",
    )(h, w, b)


def _dispatch_kernel(pos_ref, src_hbm, dst_in, dst_hbm, sem):
    del dst_in
    i = pl.program_id(0)
    tm = pos_ref.shape[0] // TOP_K

    def issue(j, carry):
        t = i * tm + j
        for k in range(TOP_K):
            pltpu.make_async_copy(src_hbm.at[t], dst_hbm.at[pos_ref[j * TOP_K + k]], sem).start()
        return carry

    lax.fori_loop(0, tm, issue, 0)
    n = tm * TOP_K
    pltpu.make_async_copy(src_hbm.at[pl.ds(0, n)], dst_hbm.at[pl.ds(0, n)], sem).wait()


def _dispatch(h_packed, pos_flat, n_rows):
    t = h_packed.shape[0]
    tm = min(DISPATCH_TM, t)
    zeros = jnp.zeros((n_rows, PACK_SLABS, LANES), U32)
    return pl.pallas_call(
        _dispatch_kernel,
        grid=(t // tm,),
        in_specs=[
            pl.BlockSpec((tm * TOP_K,), lambda i: (i,), memory_space=pltpu.SMEM),
            pl.BlockSpec(memory_space=pl.ANY),
            pl.BlockSpec(memory_space=pl.ANY),
        ],
        out_specs=pl.BlockSpec(memory_space=pl.ANY),
        out_shape=jax.ShapeDtypeStruct((n_rows, PACK_SLABS, LANES), U32),
        scratch_shapes=[pltpu.SemaphoreType.DMA(())],
        input_output_aliases={2: 0},
        compiler_params=pltpu.CompilerParams(dimension_semantics=("arbitrary",), has_side_effects=True),
        name="dispatch",
    )(pos_flat, h_packed, zeros)


def _expert_changed(be_ref, i):
    return (i == 0) | (be_ref[i] != be_ref[jnp.maximum(i - 1, 0)])


def _moe_up_kernel(be_ref, nu_ref, x_ref, wg_ref, wl_ref, bg_ref, bl_ref, o_ref, wg_bf, wl_bf, x_bf):
    i = pl.program_id(1)
    bm = o_ref.shape[0]

    @pl.when(_expert_changed(be_ref, i))
    def _():
        wg_bf[...] = wg_ref[0].astype(BF16)
        wl_bf[...] = wl_ref[0].astype(BF16)

    @pl.when(i < nu_ref[0])
    def _():
        for s in range(PACK_SLABS):
            lo, hi = _unpack_words(x_ref[pl.ds(s, bm, stride=PACK_SLABS), :])
            x_bf[:, PACK_SPAN * s:PACK_SPAN * s + LANES] = lo.astype(BF16)
            x_bf[:, PACK_SPAN * s + LANES:PACK_SPAN * (s + 1)] = hi.astype(BF16)
        x = x_bf[...]
        glu = jnp.dot(x, wg_bf[...], preferred_element_type=F32) + bg_ref[0]
        lin = jnp.dot(x, wl_bf[...], preferred_element_type=F32) + bl_ref[0]
        glu = jnp.minimum(glu, SWIGLU_LIMIT)
        lin = jnp.clip(lin, -SWIGLU_LIMIT, SWIGLU_LIMIT)
        o_ref[...] = (glu * _sigmoid(SWIGLU_ALPHA * glu) * (lin + 1.0)).astype(o_ref.dtype)

    @pl.when(i >= nu_ref[0])
    def _():
        o_ref[...] = jnp.zeros(o_ref.shape, o_ref.dtype)


def _moe_up(block_e, n_used, xs_flat, w1, b1, n_blocks):
    n_e, d, two_ff = w1.shape
    d_ff = two_ff // 2
    tf = MOE_TF
    nt = d_ff // tf
    bm = MOE_BM
    b1r = b1.reshape(n_e, 1, two_ff)
    return pl.pallas_call(
        _moe_up_kernel,
        grid_spec=pltpu.PrefetchScalarGridSpec(
            num_scalar_prefetch=2,
            grid=(nt, n_blocks),
            in_specs=[
                pl.BlockSpec((bm * PACK_SLABS, LANES), lambda n, i, be, nu: (i, 0)),
                pl.BlockSpec((1, d, tf), lambda n, i, be, nu: (be[i], 0, n)),
                pl.BlockSpec((1, d, tf), lambda n, i, be, nu: (be[i], 0, n + nt)),
                pl.BlockSpec((1, 1, tf), lambda n, i, be, nu: (be[i], 0, n)),
                pl.BlockSpec((1, 1, tf), lambda n, i, be, nu: (be[i], 0, n + nt)),
            ],
            out_specs=pl.BlockSpec((bm, tf), lambda n, i, be, nu: (i, n)),
            scratch_shapes=[pltpu.VMEM((d, tf), BF16), pltpu.VMEM((d, tf), BF16), pltpu.VMEM((bm, d), BF16)],
        ),
        out_shape=jax.ShapeDtypeStruct((n_blocks * bm, d_ff), BF16),
        compiler_params=pltpu.CompilerParams(dimension_semantics=("arbitrary", "arbitrary")),
        name="moe_up",
    )(block_e, n_used, xs_flat, w1, w1, b1r, b1r)


def _moe_down_kernel(be_ref, nu_ref, a_ref, w_ref, b_ref, o_ref, w_bf):
    i = pl.program_id(0)
    bm = a_ref.shape[0]

    @pl.when(_expert_changed(be_ref, i))
    def _():
        w_bf[...] = w_ref[0].astype(BF16)

    @pl.when(i < nu_ref[0])
    def _():
        y = jnp.dot(a_ref[...], w_bf[...], preferred_element_type=F32) + b_ref[0]
        for s in range(PACK_SLABS):
            lo = y[:, PACK_SPAN * s:PACK_SPAN * s + LANES]
            hi = y[:, PACK_SPAN * s + LANES:PACK_SPAN * (s + 1)]
            o_ref[pl.ds(s, bm, stride=PACK_SLABS), :] = _pack_words(lo, hi)

    @pl.when(i >= nu_ref[0])
    def _():
        o_ref[...] = jnp.zeros(o_ref.shape, o_ref.dtype)


def _moe_down(block_e, n_used, act, w2, b2, n_blocks):
    n_e, d_ff, d = w2.shape
    bm = MOE_BM
    return pl.pallas_call(
        _moe_down_kernel,
        grid_spec=pltpu.PrefetchScalarGridSpec(
            num_scalar_prefetch=2,
            grid=(n_blocks,),
            in_specs=[
                pl.BlockSpec((bm, d_ff), lambda i, be, nu: (i, 0)),
                pl.BlockSpec((1, d_ff, d), lambda i, be, nu: (be[i], 0, 0)),
                pl.BlockSpec((1, 1, d), lambda i, be, nu: (be[i], 0, 0)),
            ],
            out_specs=pl.BlockSpec((bm * PACK_SLABS, LANES), lambda i, be, nu: (i, 0)),
            scratch_shapes=[pltpu.VMEM((d_ff, d), BF16)],
        ),
        out_shape=jax.ShapeDtypeStruct((n_blocks * bm * PACK_SLABS, LANES), U32),
        compiler_params=pltpu.CompilerParams(dimension_semantics=("arbitrary",)),
        name="moe_down",
    )(block_e, n_used, act, w2, b2.reshape(n_e, 1, d))


def _combine_ln_kernel(alpha, pos_ref, rows_hbm, rows_flat_hbm, gate_ref, h_ref, g_ref, be_ref,
                       o_ref, obf_ref, buf_ref, sem):
    tm = h_ref.shape[0]

    def issue(j, carry):
        for k in range(TOP_K):
            dst = buf_ref.at[pl.ds(pl.multiple_of((k * tm + j) * PACK_SLABS, PACK_SLABS), PACK_SLABS), :]
            pltpu.make_async_copy(rows_hbm.at[pos_ref[j * TOP_K + k]], dst, sem).start()
        return carry

    lax.fori_loop(0, tm, issue, 0)
    n = tm * TOP_K * PACK_SLABS
    pltpu.make_async_copy(rows_flat_hbm.at[pl.ds(0, n)], buf_ref, sem).wait()

    gates = [gate_ref[:, k:k + 1] for k in range(TOP_K)]
    cols = []
    for s in range(PACK_SLABS):
        lo_acc = jnp.zeros((tm, LANES), F32)
        hi_acc = jnp.zeros((tm, LANES), F32)
        for k in range(TOP_K):
            lo, hi = _unpack_words(buf_ref[pl.ds(k * tm * PACK_SLABS + s, tm, stride=PACK_SLABS), :])
            lo_acc = lo_acc + gates[k] * lo
            hi_acc = hi_acc + gates[k] * hi
        cols.append(lo_acc)
        cols.append(hi_acc)
    ffn = jnp.concatenate(cols, axis=1)
    y = _layer_norm_rows(alpha * h_ref[...] + ffn, g_ref[...], be_ref[...])
    o_ref[...] = y
    obf_ref[...] = y.astype(BF16)


def _combine_ln(pos_flat, rows, gates, h, g, be, alpha):
    t, d = h.shape
    tm = min(COMBINE_TM, t)
    row = lambda i: (i, 0)
    fixed = lambda i: (0, 0)
    return pl.pallas_call(
        functools.partial(_combine_ln_kernel, alpha),
        grid=(t // tm,),
        in_specs=[
            pl.BlockSpec((tm * TOP_K,), lambda i: (i,), memory_space=pltpu.SMEM),
            pl.BlockSpec(memory_space=pl.ANY),
            pl.BlockSpec(memory_space=pl.ANY),
            pl.BlockSpec((tm, LANES), row),
            pl.BlockSpec((tm, d), row),
            pl.BlockSpec((1, d), fixed),
            pl.BlockSpec((1, d), fixed),
        ],
        out_specs=[pl.BlockSpec((tm, d), row), pl.BlockSpec((tm, d), row)],
        out_shape=[jax.ShapeDtypeStruct((t, d), F32), jax.ShapeDtypeStruct((t, d), BF16)],
        scratch_shapes=[pltpu.VMEM((tm * TOP_K * PACK_SLABS, LANES), U32), pltpu.SemaphoreType.DMA(())],
        compiler_params=pltpu.CompilerParams(dimension_semantics=("arbitrary",)),
        name="combine_ln",
    )(pos_flat, rows, rows.reshape(-1, LANES), gates, h, g.reshape(1, d), be.reshape(1, d))


def _moe_layer(h, h_packed, router_w, router_b, w1, b1, w2, b2, g, be, alpha):
    t, _ = h.shape
    n_e = router_w.shape[1]
    bm = MOE_BM
    n_blocks = -(-(t * TOP_K) // bm) + n_e
    idx_o, gate_o, rank_o, cnt_o = _router(h, router_w, router_b)
    idx = idx_o[:, :TOP_K]
    counts = cnt_o[0, :n_e].astype(I32)
    pcounts = (counts + bm - 1) // bm * bm
    pend = jnp.cumsum(pcounts)
    pstart = pend - pcounts
    pos = (pstart[idx] + rank_o[:, :TOP_K].astype(I32)).reshape(-1)
    block_e = jnp.minimum(jnp.searchsorted(pend, jnp.arange(n_blocks, dtype=I32) * bm, side="right"),
                          n_e - 1).astype(I32)
    n_used = (pend[-1:] // bm).astype(I32)
    xs = _dispatch(h_packed.reshape(t, PACK_SLABS, LANES), pos, n_blocks * bm)
    act = _moe_up(block_e, n_used, xs.reshape(n_blocks * bm * PACK_SLABS, LANES), w1, b1, n_blocks)
    rows = _moe_down(block_e, n_used, act, w2, b2, n_blocks)
    return _combine_ln(pos, rows.reshape(n_blocks * bm, PACK_SLABS, LANES), gate_o, h, g, be, alpha)


def kernel(x, mem, positions, attn_w_in, attn_b_in, attn_sinks, ssd_w_in, ssd_b_in, ssd_conv_w, ssd_conv_b, ssd_dt_bias, ssd_a_log, ssd_d_skip, ssd_norm_g, conf_w_in, conf_b_in, conf_dw_w, conf_dw_b, conf_ln_g, conf_ln_b, mem_w_kv, w_out, b_out, ln1_g, ln1_b, router_w, router_b, moe_w1, moe_b1, moe_w2, moe_b2, ln2_g, ln2_b):
    bsz, seq, d = x.shape
    depth = w_out.shape[0]
    t = bsz * seq
    alpha = (2 * depth) ** 0.25
    n_mem = mem.shape[1]

    posf = positions.astype(F32).reshape(t, 1)
    inv_freq = ROPE_THETA ** (-jnp.arange(ROPE_HALF, dtype=F32) / ROPE_HALF)
    lane = jnp.arange(LANES) % HEAD_DIM
    freq_row = jnp.where(lane < ROPE_DIM, inv_freq[lane % ROPE_HALF], 0.0).reshape(1, LANES)
    mem2 = mem.reshape(bsz * n_mem, d)

    h = x.reshape(t, d)
    h_in = h
    for i in range(depth):
        kind, j = i % N_MIXERS, i // N_MIXERS
        if kind == 0:
            w_in, b_in = attn_w_in[j], attn_b_in[j]
            u = _linear(h_in, w_in[:, :-XA_WIDTH], b_in[:-XA_WIDTH], BF16)
            mix = _swa_mixer(u, posf, attn_sinks[j], freq_row, bsz, seq)
        elif kind == 1:
            w_in, b_in = ssd_w_in[j], ssd_b_in[j]
            z = _linear(h_in, w_in[:, :MIX_WIDTH], b_in[:MIX_WIDTH], BF16)
            xbc = _linear(h_in, w_in[:, MIX_WIDTH:MIX_WIDTH + SSD_CONV_CH],
                          b_in[MIX_WIDTH:MIX_WIDTH + SSD_CONV_CH], F32)
            dt_lo = MIX_WIDTH + SSD_CONV_CH
            pad = LANES - SSD_HEADS
            dt_raw = _linear(h_in, jnp.pad(w_in[:, dt_lo:dt_lo + SSD_HEADS], ((0, 0), (0, pad))),
                             jnp.pad(b_in[dt_lo:dt_lo + SSD_HEADS], (0, pad)), F32)
            mix = _ssd_mixer(z, xbc, dt_raw, ssd_conv_w[j], ssd_conv_b[j], ssd_dt_bias[j], ssd_a_log[j],
                             ssd_d_skip[j], ssd_norm_g[j], bsz, seq)
        else:
            w_in, b_in = conf_w_in[j], conf_b_in[j]
            u = _linear(h_in, w_in[:, :-XA_WIDTH], b_in[:-XA_WIDTH], BF16)
            mix = _conformer_mixer(u, conf_dw_w[j], conf_dw_b[j], conf_ln_g[j], conf_ln_b[j], bsz, seq)
        qm = _linear(h_in, w_in[:, -XA_WIDTH:], b_in[-XA_WIDTH:], BF16)
        kv = _linear(mem2, mem_w_kv[i], jnp.zeros((2 * XA_WIDTH,), F32), BF16)
        xa = _memory_attention(qm, kv, bsz, seq)
        h1, _, h1_packed = _outproj_ln(mix, xa, w_out[i], b_out[i], h, ln1_g[i], ln1_b[i], alpha)
        h, h_in = _moe_layer(h1, h1_packed, router_w[i], router_b[i], moe_w1[i], moe_b1[i],
                             moe_w2[i], moe_b2[i], ln2_g[i], ln2_b[i], alpha)
    return h.reshape(bsz, seq, d)
```

```python
import functools
import math

import jax
import jax.numpy as jnp
from jax import lax
from jax.experimental import pallas as pl
from jax.experimental.pallas import tpu as pltpu

F32 = jnp.float32
BF16 = jnp.bfloat16
U32 = jnp.uint32
I32 = jnp.int32

LANES = 128
PACK_SLABS = 8
PACK_SPAN = 2 * LANES

D_MODEL = 2048
N_MIXERS = 3
MIX_WIDTH = 1536
XA_HEADS = 4
XA_HEAD_DIM = 128
XA_WIDTH = XA_HEADS * XA_HEAD_DIM
HEAD_DIM = 64
N_Q_HEADS = MIX_WIDTH // HEAD_DIM
N_KV_HEADS = 3
Q_PER_KV = N_Q_HEADS // N_KV_HEADS
WINDOW = 128
ROPE_DIM = HEAD_DIM // 4
ROPE_HALF = ROPE_DIM // 2
ROPE_THETA = 500000.0
SSD_HEAD_DIM = 64
SSD_HEADS = MIX_WIDTH // SSD_HEAD_DIM
SSD_GROUPS = 4
SSD_HPG = SSD_HEADS // SSD_GROUPS
SSD_GROUP_WIDTH = SSD_HPG * SSD_HEAD_DIM
SSD_STATE = 128
SSD_CONV = 4
SSD_CHUNK = 128
SSD_BC = SSD_GROUPS * SSD_STATE
SSD_CONV_CH = MIX_WIDTH + 2 * SSD_BC
CONF_WIDTH = 31
CONF_HALO = 32
SSD_HALO = 8
TOP_K = 4
D_FF = 1024
SWIGLU_LIMIT = 7.0
SWIGLU_ALPHA = 1.702
LN_EPS = 1e-5

MOE_BM = 512
MOE_TF = 512
ROUTER_TM = 256
COMBINE_TM = 256
DISPATCH_TM = 512
DISPATCH_UNROLL = 8
COMBINE_UNROLL = 8
CONF_TR = 256
XA_TQ = 512
OUT_TM = 512

HI_MASK = 0xFFFF0000


def _sigmoid(x):
    return 1.0 / (1.0 + jnp.exp(-x))


def _silu(x):
    return x * _sigmoid(x)


def _pick_tile(n, cap):
    best = LANES
    for t in range(LANES, cap + 1, LANES):
        if n % t == 0:
            best = t
    return best


def _pack_words(a, b):
    a = a.astype(BF16).astype(F32)
    b = b.astype(BF16).astype(F32)
    return (pltpu.bitcast(a, U32) >> 16) | (pltpu.bitcast(b, U32) & jnp.uint32(HI_MASK))


def _unpack_words(w):
    lo = pltpu.bitcast(w << 16, F32)
    hi = pltpu.bitcast(w & jnp.uint32(HI_MASK), F32)
    return lo, hi


def _layer_norm_rows(y, g, b):
    mu = jnp.mean(y, axis=-1, keepdims=True)
    d = y - mu
    var = jnp.mean(d * d, axis=-1, keepdims=True)
    return d * lax.rsqrt(var + LN_EPS) * g + b


def _linear_kernel(x_ref, w_ref, b_ref, o_ref):
    x = x_ref[...].astype(BF16)
    acc = jnp.dot(x, w_ref[...], preferred_element_type=F32) + b_ref[...]
    o_ref[...] = acc.astype(o_ref.dtype)


def _linear(x, w, b, out_dtype, tm=512, tn_cap=1024):
    m, k = x.shape
    n = w.shape[1]
    tn = _pick_tile(n, tn_cap)
    tm = min(tm, m)
    return pl.pallas_call(
        _linear_kernel,
        grid=(n // tn, m // tm),
        in_specs=[
            pl.BlockSpec((tm, k), lambda j, i: (i, 0)),
            pl.BlockSpec((k, tn), lambda j, i: (0, j)),
            pl.BlockSpec((1, tn), lambda j, i: (0, j)),
        ],
        out_specs=pl.BlockSpec((tm, tn), lambda j, i: (i, j)),
        out_shape=jax.ShapeDtypeStruct((m, n), out_dtype),
        compiler_params=pltpu.CompilerParams(dimension_semantics=("arbitrary", "arbitrary")),
        name="linear",
    )(x, w.astype(BF16), b.reshape(1, n).astype(F32))


def _outproj_ln_kernel(alpha, mix_ref, xa_ref, wm_ref, wx_ref, b_ref, h_ref, g_ref, be_ref,
                       o_ref, obf_ref, opk_ref):
    sub = jnp.dot(mix_ref[...], wm_ref[...], preferred_element_type=F32)
    sub = sub + jnp.dot(xa_ref[...], wx_ref[...], preferred_element_type=F32) + b_ref[...]
    y = _layer_norm_rows(alpha * h_ref[...] + sub, g_ref[...], be_ref[...])
    o_ref[...] = y
    obf_ref[...] = y.astype(BF16)
    tm = y.shape[0]
    for s in range(PACK_SLABS):
        lo = y[:, PACK_SPAN * s:PACK_SPAN * s + LANES]
        hi = y[:, PACK_SPAN * s + LANES:PACK_SPAN * (s + 1)]
        opk_ref[pl.ds(s, tm, stride=PACK_SLABS), :] = _pack_words(lo, hi)


def _outproj_ln(mix, xa, w_out, b_out, h, g, be, alpha):
    t, d = h.shape
    tm = min(OUT_TM, t)
    wm = w_out[:MIX_WIDTH].astype(BF16)
    wx = w_out[MIX_WIDTH:].astype(BF16)
    row = lambda i: (i, 0)
    fixed = lambda i: (0, 0)
    return pl.pallas_call(
        functools.partial(_outproj_ln_kernel, alpha),
        grid=(t // tm,),
        in_specs=[
            pl.BlockSpec((tm, MIX_WIDTH), row),
            pl.BlockSpec((tm, XA_WIDTH), row),
            pl.BlockSpec((MIX_WIDTH, d), fixed),
            pl.BlockSpec((XA_WIDTH, d), fixed),
            pl.BlockSpec((1, d), fixed),
            pl.BlockSpec((tm, d), row),
            pl.BlockSpec((1, d), fixed),
            pl.BlockSpec((1, d), fixed),
        ],
        out_specs=[
            pl.BlockSpec((tm, d), row),
            pl.BlockSpec((tm, d), row),
            pl.BlockSpec((tm * PACK_SLABS, LANES), row),
        ],
        out_shape=[
            jax.ShapeDtypeStruct((t, d), F32),
            jax.ShapeDtypeStruct((t, d), BF16),
            jax.ShapeDtypeStruct((t * PACK_SLABS, LANES), U32),
        ],
        compiler_params=pltpu.CompilerParams(dimension_semantics=("arbitrary",)),
        name="outproj_ln",
    )(mix, xa, wm, wx, b_out.reshape(1, d), h, g.reshape(1, d), be.reshape(1, d))


def _xa_kernel(q_ref, k_ref, v_ref, o_ref):
    scale = XA_HEAD_DIM ** -0.5
    outs = []
    for hd in range(XA_HEADS):
        sl = slice(hd * XA_HEAD_DIM, (hd + 1) * XA_HEAD_DIM)
        s = lax.dot_general(q_ref[:, sl], k_ref[:, sl], (((1,), (1,)), ((), ())),
                            preferred_element_type=F32) * scale
        m = jnp.max(s, axis=-1, keepdims=True)
        p = jnp.exp(s - m)
        den = jnp.sum(p, axis=-1, keepdims=True)
        o = jnp.dot(p.astype(BF16), v_ref[:, sl], preferred_element_type=F32)
        outs.append(o / den)
    o_ref[...] = jnp.concatenate(outs, axis=-1).astype(o_ref.dtype)


def _memory_attention(qm, kv, bsz, seq):
    n_mem = kv.shape[0] // bsz
    tq = min(XA_TQ, seq)
    nq = seq // tq
    return pl.pallas_call(
        _xa_kernel,
        grid=(bsz, nq),
        in_specs=[
            pl.BlockSpec((tq, XA_WIDTH), lambda b, i: (b * nq + i, 0)),
            pl.BlockSpec((n_mem, XA_WIDTH), lambda b, i: (b, 0)),
            pl.BlockSpec((n_mem, XA_WIDTH), lambda b, i: (b, 1)),
        ],
        out_specs=pl.BlockSpec((tq, XA_WIDTH), lambda b, i: (b * nq + i, 0)),
        out_shape=jax.ShapeDtypeStruct((bsz * seq, XA_WIDTH), BF16),
        compiler_params=pltpu.CompilerParams(dimension_semantics=("arbitrary", "arbitrary")),
        name="memory_attention",
    )(qm, kv, kv)


def _rope_tables(pos, freq_row):
    ang = pos * freq_row
    c = jnp.cos(ang)
    s = jnp.sin(ang)
    d = lax.broadcasted_iota(I32, ang.shape, 1) % HEAD_DIM
    cmul = jnp.where(d < ROPE_DIM, c, 1.0)
    s_up = jnp.where(d < ROPE_HALF, -s, 0.0)
    s_dn = jnp.where((d >= ROPE_HALF) & (d < ROPE_DIM), s, 0.0)
    return cmul, s_up, s_dn


def _rope_chunk(t, tabs):
    cmul, s_up, s_dn = tabs
    up = pltpu.roll(t, LANES - ROPE_HALF, axis=1)
    dn = pltpu.roll(t, ROPE_HALF, axis=1)
    return t * cmul + up * s_up + dn * s_dn


def _swa_kernel(sinks_ref, q_ref, kvp_ref, kvc_ref, posp_ref, posc_ref, freq_ref, o_ref):
    n = pl.program_id(1)
    w = WINDOW
    freq = freq_ref[...]
    tab_c = _rope_tables(posc_ref[...], freq)
    tab_p = _rope_tables(posp_ref[...], freq)
    lane = lax.broadcasted_iota(I32, (w, LANES), 1)
    first_half = lane < HEAD_DIM

    def rope_kv(kv_ref, tabs):
        c0 = _rope_chunk(kv_ref[:, 0:LANES].astype(F32), tabs)
        c1 = kv_ref[:, LANES:2 * LANES].astype(F32)
        c1 = jnp.where(first_half, _rope_chunk(c1, tabs), c1)
        c2 = kv_ref[:, 2 * LANES:3 * LANES].astype(F32)
        return jnp.concatenate([c0, c1, c2], axis=1)

    kv_c = rope_kv(kvc_ref, tab_c)
    kv_p = rope_kv(kvp_ref, tab_p)
    kv = jnp.concatenate([kv_p, kv_c], axis=0).astype(BF16)

    scale = HEAD_DIM ** -0.5
    qs = []
    for j in range(MIX_WIDTH // LANES):
        qj = _rope_chunk(q_ref[:, j * LANES:(j + 1) * LANES].astype(F32), tab_c) * scale
        qs.append(qj.astype(BF16))

    rows = Q_PER_KV * w
    qi = lax.broadcasted_iota(I32, (rows, 2 * w), 0) % w
    sj = lax.broadcasted_iota(I32, (rows, 2 * w), 1)
    rel = qi + w - sj
    mask = (rel >= 0) & (rel < w) & ((n > 0) | (sj >= w))

    outs = []
    for g in range(N_KV_HEADS):
        k_g = kv[:, g * HEAD_DIM:(g + 1) * HEAD_DIM]
        v_g = kv[:, (N_KV_HEADS + g) * HEAD_DIM:(N_KV_HEADS + g + 1) * HEAD_DIM]
        q_parts = []
        sink_parts = []
        for r in range(Q_PER_KV):
            hd = g * Q_PER_KV + r
            chunk = qs[hd // 2]
            q_parts.append(chunk[:, (hd % 2) * HEAD_DIM:(hd % 2 + 1) * HEAD_DIM])
            sink_parts.append(jnp.full((w, 1), sinks_ref[hd], F32))
        q_g = jnp.concatenate(q_parts, axis=0)
        sink = jnp.concatenate(sink_parts, axis=0)
        s = lax.dot_general(q_g, k_g, (((1,), (1,)), ((), ())), preferred_element_type=F32)
        s = jnp.where(mask, s, -jnp.inf)
        m = jnp.maximum(jnp.max(s, axis=-1, keepdims=True), sink)
        p = jnp.exp(s - m)
        den = jnp.sum(p, axis=-1, keepdims=True) + jnp.exp(sink - m)
        o = jnp.dot(p.astype(BF16), v_g, preferred_element_type=F32) / den
        for r in range(Q_PER_KV):
            outs.append(o[r * w:(r + 1) * w, :])
    o_ref[...] = jnp.concatenate(outs, axis=1).astype(o_ref.dtype)


def _swa_mixer(u, posf, sinks, freq_row, bsz, seq):
    nb = seq // WINDOW
    kv_w = 2 * N_KV_HEADS * HEAD_DIM
    kv_col = MIX_WIDTH // kv_w
    cur = lambda b, n, s: (b * nb + n, 0)
    prev = lambda b, n, s: (b * nb + jnp.maximum(n - 1, 0), 0)
    return pl.pallas_call(
        _swa_kernel,
        grid_spec=pltpu.PrefetchScalarGridSpec(
            num_scalar_prefetch=1,
            grid=(bsz, nb),
            in_specs=[
                pl.BlockSpec((WINDOW, MIX_WIDTH), cur),
                pl.BlockSpec((WINDOW, kv_w), lambda b, n, s: (b * nb + jnp.maximum(n - 1, 0), kv_col)),
                pl.BlockSpec((WINDOW, kv_w), lambda b, n, s: (b * nb + n, kv_col)),
                pl.BlockSpec((WINDOW, 1), prev),
                pl.BlockSpec((WINDOW, 1), cur),
                pl.BlockSpec((1, LANES), lambda b, n, s: (0, 0)),
            ],
            out_specs=pl.BlockSpec((WINDOW, MIX_WIDTH), cur),
        ),
        out_shape=jax.ShapeDtypeStruct((bsz * seq, MIX_WIDTH), BF16),
        compiler_params=pltpu.CompilerParams(dimension_semantics=("arbitrary", "arbitrary")),
        name="swa",
    )(sinks.astype(F32), u, u, u, posf, posf, freq_row)


def _split3(x):
    p0 = x.astype(BF16)
    r = x - p0.astype(F32)
    p1 = r.astype(BF16)
    p2 = (r - p1.astype(F32)).astype(BF16)
    return p0, p1, p2


def _expand_heads(arr, g):
    rows = arr.shape[0]
    parts = [jnp.broadcast_to(arr[:, g * SSD_HPG + r:g * SSD_HPG + r + 1], (rows, SSD_HEAD_DIM))
             for r in range(SSD_HPG)]
    return jnp.concatenate(parts, axis=1)


def _ssd_kernel(z_ref, xbc_ref, dt_ref, cw_ref, cb_ref, dtb_ref, alog_ref, dskip_ref, ng_ref,
                o_ref, ext_ref, state_ref):
    c = pl.program_id(1)
    q = SSD_CHUNK

    @pl.when(c == 0)
    def _():
        ext_ref[0:SSD_HALO, :] = jnp.zeros((SSD_HALO, SSD_CONV_CH), F32)
        state_ref[...] = jnp.zeros(state_ref.shape, F32)

    ext_ref[SSD_HALO:SSD_HALO + q, :] = xbc_ref[...].astype(F32)
    conv = cb_ref[...]
    for k in range(SSD_CONV):
        conv = conv + cw_ref[k:k + 1, :] * ext_ref[pl.ds(SSD_HALO - (SSD_CONV - 1) + k, q), :]
    ext_ref[0:SSD_HALO, :] = ext_ref[q:q + SSD_HALO, :]
    xbc = _silu(conv)

    lane = lax.broadcasted_iota(I32, (q, LANES), 1)
    x = dt_ref[...] + dtb_ref[...]
    dt = jnp.maximum(x, 0.0) + jnp.log1p(jnp.exp(-jnp.abs(x)))
    a = jnp.where(lane < SSD_HEADS, dt * (-jnp.exp(alog_ref[...])), 0.0)

    row = lax.broadcasted_iota(I32, (q, q), 0)
    col = lax.broadcasted_iota(I32, (q, q), 1)
    causal = row >= col
    tri = causal.astype(BF16)
    a_cs = jnp.zeros((q, LANES), F32)
    for piece in _split3(a):
        a_cs = a_cs + jnp.dot(tri, piece, preferred_element_type=F32)
    a_cs_t = a_cs.T
    a_last = jnp.broadcast_to(a_cs[q - 1:q, :], (q, LANES))

    ys = []
    for g in range(SSD_GROUPS):
        xs_g = xbc[:, g * SSD_GROUP_WIDTH:(g + 1) * SSD_GROUP_WIDTH]
        b_g = xbc[:, MIX_WIDTH + g * SSD_STATE:MIX_WIDTH + (g + 1) * SSD_STATE].astype(BF16)
        c_g = xbc[:, MIX_WIDTH + SSD_BC + g * SSD_STATE:MIX_WIDTH + SSD_BC + (g + 1) * SSD_STATE].astype(BF16)
        cb = lax.dot_general(c_g, b_g, (((1,), (1,)), ((), ())), preferred_element_type=F32)
        acs_e = _expand_heads(a_cs, g)
        last_e = _expand_heads(a_last, g)
        xdt = xs_g * _expand_heads(dt, g)
        xdt_bf = xdt.astype(BF16)

        diag = []
        for r in range(SSD_HPG):
            hd = g * SSD_HPG + r
            seg = a_cs[:, hd:hd + 1] - a_cs_t[hd:hd + 1, :]
            decay = jnp.exp(jnp.where(causal, seg, -jnp.inf))
            mat = (cb * decay).astype(BF16)
            diag.append(jnp.dot(mat, xdt_bf[:, r * SSD_HEAD_DIM:(r + 1) * SSD_HEAD_DIM],
                                preferred_element_type=F32))
        y_diag = jnp.concatenate(diag, axis=1)

        prev = state_ref[g]
        y_off = jnp.dot(c_g, prev.astype(BF16), preferred_element_type=F32) * jnp.exp(acs_e)
        wgt = (xdt * jnp.exp(last_e - acs_e)).astype(BF16)
        upd = lax.dot_general(b_g, wgt, (((0,), (0,)), ((), ())), preferred_element_type=F32)
        state_ref[g] = prev * jnp.exp(last_e[0:1, :]) + upd
        ys.append(y_diag + y_off + dskip_ref[:, g * SSD_GROUP_WIDTH:(g + 1) * SSD_GROUP_WIDTH] * xs_g)

    y = jnp.concatenate(ys, axis=1) * _silu(z_ref[...].astype(F32))
    ms = jnp.mean(y * y, axis=-1, keepdims=True)
    o_ref[...] = (y * lax.rsqrt(ms + LN_EPS) * ng_ref[...]).astype(o_ref.dtype)


def _ssd_mixer(z, xbc, dt_raw, conv_w, conv_b, dt_bias, a_log, d_skip, norm_g, bsz, seq):
    nc = seq // SSD_CHUNK
    pad = LANES - SSD_HEADS
    row = lambda b, c: (b * nc + c, 0)
    fixed = lambda b, c: (0, 0)
    return pl.pallas_call(
        _ssd_kernel,
        grid=(bsz, nc),
        in_specs=[
            pl.BlockSpec((SSD_CHUNK, MIX_WIDTH), row),
            pl.BlockSpec((SSD_CHUNK, SSD_CONV_CH), row),
            pl.BlockSpec((SSD_CHUNK, LANES), row),
            pl.BlockSpec((SSD_CONV, SSD_CONV_CH), fixed),
            pl.BlockSpec((1, SSD_CONV_CH), fixed),
            pl.BlockSpec((1, LANES), fixed),
            pl.BlockSpec((1, LANES), fixed),
            pl.BlockSpec((1, MIX_WIDTH), fixed),
            pl.BlockSpec((1, MIX_WIDTH), fixed),
        ],
        out_specs=pl.BlockSpec((SSD_CHUNK, MIX_WIDTH), row),
        out_shape=jax.ShapeDtypeStruct((bsz * seq, MIX_WIDTH), BF16),
        scratch_shapes=[
            pltpu.VMEM((SSD_HALO + SSD_CHUNK, SSD_CONV_CH), F32),
            pltpu.VMEM((SSD_GROUPS, SSD_STATE, SSD_GROUP_WIDTH), F32),
        ],
        compiler_params=pltpu.CompilerParams(dimension_semantics=("arbitrary", "arbitrary")),
        name="ssd",
    )(z, xbc, dt_raw, conv_w, conv_b.reshape(1, -1),
      jnp.pad(dt_bias, (0, pad)).reshape(1, LANES), jnp.pad(a_log, (0, pad)).reshape(1, LANES),
      jnp.repeat(d_skip, SSD_HEAD_DIM).reshape(1, MIX_WIDTH), norm_g.reshape(1, MIX_WIDTH))


def _conformer_kernel(a_ref, b_ref, w_ref, wb_ref, g_ref, be_ref, o_ref, ext_ref, acc_ref):
    i = pl.program_id(1)
    tr = a_ref.shape[0]

    @pl.when(i == 0)
    def _():
        ext_ref[0:CONF_HALO, :] = jnp.zeros((CONF_HALO, MIX_WIDTH), F32)

    ext_ref[CONF_HALO:CONF_HALO + tr, :] = a_ref[...].astype(F32) * _sigmoid(b_ref[...].astype(F32))
    first = CONF_HALO - (CONF_WIDTH - 1)
    for j in range(MIX_WIDTH // LANES):
        cs = slice(j * LANES, (j + 1) * LANES)
        acc = jnp.broadcast_to(wb_ref[:, cs], (tr, LANES))
        for k in range(CONF_WIDTH):
            acc = acc + w_ref[k:k + 1, cs] * ext_ref[pl.ds(first + k, tr), cs]
        acc_ref[:, cs] = acc
    ext_ref[0:CONF_HALO, :] = ext_ref[tr:tr + CONF_HALO, :]
    y = _layer_norm_rows(acc_ref[...], g_ref[...], be_ref[...])
    o_ref[...] = _silu(y).astype(o_ref.dtype)


def _conformer_mixer(u, dw_w, dw_b, ln_g, ln_b, bsz, seq):
    tr = min(CONF_TR, seq)
    nr = seq // tr
    fixed = lambda b, i: (0, 0)
    return pl.pallas_call(
        _conformer_kernel,
        grid=(bsz, nr),
        in_specs=[
            pl.BlockSpec((tr, MIX_WIDTH), lambda b, i: (b * nr + i, 0)),
            pl.BlockSpec((tr, MIX_WIDTH), lambda b, i: (b * nr + i, 1)),
            pl.BlockSpec((CONF_WIDTH, MIX_WIDTH), fixed),
            pl.BlockSpec((1, MIX_WIDTH), fixed),
            pl.BlockSpec((1, MIX_WIDTH), fixed),
            pl.BlockSpec((1, MIX_WIDTH), fixed),
        ],
        out_specs=pl.BlockSpec((tr, MIX_WIDTH), lambda b, i: (b * nr + i, 0)),
        out_shape=jax.ShapeDtypeStruct((bsz * seq, MIX_WIDTH), BF16),
        scratch_shapes=[
            pltpu.VMEM((CONF_HALO + tr, MIX_WIDTH), F32),
            pltpu.VMEM((tr, MIX_WIDTH), F32),
        ],
        compiler_params=pltpu.CompilerParams(dimension_semantics=("arbitrary", "arbitrary")),
        name="conformer",
    )(u, u, dw_w, dw_b.reshape(1, -1), ln_g.reshape(1, -1), ln_b.reshape(1, -1))


def _router_kernel(n_experts, h_ref, w_ref, wlo_ref, b_ref, idx_ref, gate_ref, rank_ref, cnt_ref, run_ref):
    i = pl.program_id(0)
    tm = h_ref.shape[0]

    @pl.when(i == 0)
    def _():
        run_ref[...] = jnp.zeros(run_ref.shape, F32)

    x = h_ref[...]
    x_hi = x.astype(BF16)
    x_lo = (x - x_hi.astype(F32)).astype(BF16)
    logits = (jnp.dot(x_hi, w_ref[...], preferred_element_type=F32)
              + jnp.dot(x_lo, w_ref[...], preferred_element_type=F32)
              + jnp.dot(x_hi, wlo_ref[...], preferred_element_type=F32)) + b_ref[...]
    lane = lax.broadcasted_iota(I32, (tm, LANES), 1).astype(F32)
    cur = jnp.where(lane < n_experts, logits, -jnp.inf)
    vals, idxs, hots = [], [], []
    for _ in range(TOP_K):
        m = jnp.max(cur, axis=-1, keepdims=True)
        ix = jnp.min(jnp.where(cur == m, lane, float(LANES)), axis=-1, keepdims=True)
        hot = lane == ix
        vals.append(m)
        idxs.append(ix)
        hots.append(hot)
        cur = jnp.where(hot, -jnp.inf, cur)
    exps = [jnp.exp(v - vals[0]) for v in vals]
    den = exps[0]
    for e in exps[1:]:
        den = den + e

    r = lax.broadcasted_iota(I32, (tm, tm), 0)
    c = lax.broadcasted_iota(I32, (tm, tm), 1)
    before = (r > c).astype(BF16)
    base = run_ref[0:1, :]
    ranks = []
    for hot in hots:
        hot_f = hot.astype(F32)
        earlier = jnp.dot(before, hot.astype(BF16), preferred_element_type=F32)
        ranks.append(jnp.sum(hot_f * (earlier + base), axis=-1, keepdims=True))
        base = base + jnp.sum(hot_f, axis=0, keepdims=True)
    run_ref[0:1, :] = base
    cnt_ref[...] = jnp.broadcast_to(base, cnt_ref.shape)

    idx_out = jnp.zeros((tm, LANES), F32)
    gate_out = jnp.zeros((tm, LANES), F32)
    rank_out = jnp.zeros((tm, LANES), F32)
    for k in range(TOP_K):
        sel = lane == k
        idx_out = jnp.where(sel, idxs[k], idx_out)
        gate_out = jnp.where(sel, exps[k] / den, gate_out)
        rank_out = jnp.where(sel, ranks[k], rank_out)
    idx_ref[...] = idx_out.astype(I32)
    gate_ref[...] = gate_out
    rank_ref[...] = rank_out


def _router(h, router_w, router_b):
    t, d = h.shape
    n_experts = router_w.shape[1]
    tm = min(ROUTER_TM, t)
    w = jnp.pad(router_w.astype(F32), ((0, 0), (0, LANES - n_experts)))
    w_hi = w.astype(BF16)
    w_lo = (w - w_hi.astype(F32)).astype(BF16)
    b = jnp.pad(router_b.astype(F32), (0, LANES - n_experts)).reshape(1, LANES)
    row = lambda i: (i, 0)
    fixed = lambda i: (0, 0)
    return pl.pallas_call(
        functools.partial(_router_kernel, n_experts),
        grid=(t // tm,),
        in_specs=[pl.BlockSpec((tm, d), row), pl.BlockSpec((d, LANES), fixed), pl.BlockSpec((d, LANES), fixed),
                  pl.BlockSpec((1, LANES), fixed)],
        out_specs=[pl.BlockSpec((tm, LANES), row), pl.BlockSpec((tm, LANES), row),
                   pl.BlockSpec((tm, LANES), row), pl.BlockSpec((8, LANES), fixed)],
        out_shape=[jax.ShapeDtypeStruct((t, LANES), I32), jax.ShapeDtypeStruct((t, LANES), F32),
                   jax.ShapeDtypeStruct((t, LANES), F32), jax.ShapeDtypeStruct((8, LANES), F32)],
        scratch_shapes=[pltpu.VMEM((8, LANES), F32)],
        compiler_params=pltpu.CompilerParams(dimension_semantics=("arbitrary",)),
        name="router",
    )(h, w_hi, w_lo, b)


def _dispatch_kernel(pos_ref, src_ref, dst_in, dst_hbm, sem):
    del dst_in
    tm = src_ref.shape[0]

    def issue(jo, carry):
        for ji in range(DISPATCH_UNROLL):
            j = jo * DISPATCH_UNROLL + ji
            for k in range(TOP_K):
                pltpu.make_async_copy(src_ref.at[j], dst_hbm.at[pos_ref[j * TOP_K + k]], sem).start()
        return carry

    lax.fori_loop(0, tm // DISPATCH_UNROLL, issue, 0)
    n = tm * TOP_K
    pltpu.make_async_copy(dst_hbm.at[pl.ds(0, n)], dst_hbm.at[pl.ds(0, n)], sem).wait()


def _dispatch(h_packed, pos_flat, n_rows):
    t = h_packed.shape[0]
    tm = min(DISPATCH_TM, t)
    zeros = jnp.zeros((n_rows, PACK_SLABS, LANES), U32)
    return pl.pallas_call(
        _dispatch_kernel,
        grid=(t // tm,),
        in_specs=[
            pl.BlockSpec((tm * TOP_K,), lambda i: (i,), memory_space=pltpu.SMEM),
            pl.BlockSpec((tm, PACK_SLABS, LANES), lambda i: (i, 0, 0)),
            pl.BlockSpec(memory_space=pl.ANY),
        ],
        out_specs=pl.BlockSpec(memory_space=pl.ANY),
        out_shape=jax.ShapeDtypeStruct((n_rows, PACK_SLABS, LANES), U32),
        scratch_shapes=[pltpu.SemaphoreType.DMA(())],
        input_output_aliases={2: 0},
        compiler_params=pltpu.CompilerParams(dimension_semantics=("arbitrary",), has_side_effects=True),
        name="dispatch",
    )(pos_flat, h_packed, zeros)


def _expert_changed(be_ref, i):
    return (i == 0) | (be_ref[i] != be_ref[jnp.maximum(i - 1, 0)])


def _moe_up_kernel(be_ref, nu_ref, x_ref, wg_ref, wl_ref, bg_ref, bl_ref, o_ref, wg_bf, wl_bf, x_bf):
    i = pl.program_id(1)
    bm = o_ref.shape[0]

    @pl.when(_expert_changed(be_ref, i))
    def _():
        wg_bf[...] = wg_ref[0].astype(BF16)
        wl_bf[...] = wl_ref[0].astype(BF16)

    @pl.when(i < nu_ref[0])
    def _():
        for s in range(PACK_SLABS):
            lo, hi = _unpack_words(x_ref[pl.ds(s, bm, stride=PACK_SLABS), :])
            x_bf[:, PACK_SPAN * s:PACK_SPAN * s + LANES] = lo.astype(BF16)
            x_bf[:, PACK_SPAN * s + LANES:PACK_SPAN * (s + 1)] = hi.astype(BF16)
        x = x_bf[...]
        glu = jnp.dot(x, wg_bf[...], preferred_element_type=F32) + bg_ref[0]
        lin = jnp.dot(x, wl_bf[...], preferred_element_type=F32) + bl_ref[0]
        glu = jnp.minimum(glu, SWIGLU_LIMIT)
        lin = jnp.clip(lin, -SWIGLU_LIMIT, SWIGLU_LIMIT)
        o_ref[...] = (glu * _sigmoid(SWIGLU_ALPHA * glu) * (lin + 1.0)).astype(o_ref.dtype)

    @pl.when(i >= nu_ref[0])
    def _():
        o_ref[...] = jnp.zeros(o_ref.shape, o_ref.dtype)


def _moe_up(block_e, n_used, xs_flat, w1, b1, layer, n_blocks):
    n_layers, n_e, d, two_ff = w1.shape
    d_ff = two_ff // 2
    tf = MOE_TF
    nt = d_ff // tf
    bm = MOE_BM
    b1r = b1.reshape(n_layers, n_e, 1, two_ff)
    return pl.pallas_call(
        _moe_up_kernel,
        grid_spec=pltpu.PrefetchScalarGridSpec(
            num_scalar_prefetch=2,
            grid=(nt, n_blocks),
            in_specs=[
                pl.BlockSpec((bm * PACK_SLABS, LANES), lambda n, i, be, nu: (i, 0)),
                pl.BlockSpec((None, 1, d, tf), lambda n, i, be, nu: (layer, be[i], 0, n)),
                pl.BlockSpec((None, 1, d, tf), lambda n, i, be, nu: (layer, be[i], 0, n + nt)),
                pl.BlockSpec((None, 1, 1, tf), lambda n, i, be, nu: (layer, be[i], 0, n)),
                pl.BlockSpec((None, 1, 1, tf), lambda n, i, be, nu: (layer, be[i], 0, n + nt)),
            ],
            out_specs=pl.BlockSpec((bm, tf), lambda n, i, be, nu: (i, n)),
            scratch_shapes=[pltpu.VMEM((d, tf), BF16), pltpu.VMEM((d, tf), BF16), pltpu.VMEM((bm, d), BF16)],
        ),
        out_shape=jax.ShapeDtypeStruct((n_blocks * bm, d_ff), BF16),
        compiler_params=pltpu.CompilerParams(dimension_semantics=("arbitrary", "arbitrary")),
        name="moe_up",
    )(block_e, n_used, xs_flat, w1, w1, b1r, b1r)


def _moe_down_kernel(be_ref, nu_ref, a_ref, w_ref, b_ref, o_ref, w_bf):
    i = pl.program_id(0)
    bm = a_ref.shape[0]

    @pl.when(_expert_changed(be_ref, i))
    def _():
        w_bf[...] = w_ref[0].astype(BF16)

    @pl.when(i < nu_ref[0])
    def _():
        y = jnp.dot(a_ref[...], w_bf[...], preferred_element_type=F32) + b_ref[0]
        for s in range(PACK_SLABS):
            lo = y[:, PACK_SPAN * s:PACK_SPAN * s + LANES]
            hi = y[:, PACK_SPAN * s + LANES:PACK_SPAN * (s + 1)]
            o_ref[pl.ds(s, bm, stride=PACK_SLABS), :] = _pack_words(lo, hi)

    @pl.when(i >= nu_ref[0])
    def _():
        o_ref[...] = jnp.zeros(o_ref.shape, o_ref.dtype)


def _moe_down(block_e, n_used, act, w2, b2, layer, n_blocks):
    n_layers, n_e, d_ff, d = w2.shape
    bm = MOE_BM
    return pl.pallas_call(
        _moe_down_kernel,
        grid_spec=pltpu.PrefetchScalarGridSpec(
            num_scalar_prefetch=2,
            grid=(n_blocks,),
            in_specs=[
                pl.BlockSpec((bm, d_ff), lambda i, be, nu: (i, 0)),
                pl.BlockSpec((None, 1, d_ff, d), lambda i, be, nu: (layer, be[i], 0, 0)),
                pl.BlockSpec((None, 1, 1, d), lambda i, be, nu: (layer, be[i], 0, 0)),
            ],
            out_specs=pl.BlockSpec((bm * PACK_SLABS, LANES), lambda i, be, nu: (i, 0)),
            scratch_shapes=[pltpu.VMEM((d_ff, d), BF16)],
        ),
        out_shape=jax.ShapeDtypeStruct((n_blocks * bm * PACK_SLABS, LANES), U32),
        compiler_params=pltpu.CompilerParams(dimension_semantics=("arbitrary",)),
        name="moe_down",
    )(block_e, n_used, act, w2, b2.reshape(n_layers, n_e, 1, d))


def _combine_ln_kernel(alpha, pos_ref, pos_next_ref, rows_hbm, rows_flat_hbm, gate_ref, h_ref, g_ref,
                       be_ref, o_ref, obf_ref, buf_ref, sem):
    i = pl.program_id(0)
    tm = h_ref.shape[0]
    n = tm * TOP_K * PACK_SLABS
    slot = i % 2

    def issue_tile(p_ref, to_slot):
        base = to_slot * n

        def body(jo, carry):
            for ji in range(COMBINE_UNROLL):
                j = jo * COMBINE_UNROLL + ji
                for k in range(TOP_K):
                    off = pl.multiple_of(base + (k * tm + j) * PACK_SLABS, PACK_SLABS)
                    pltpu.make_async_copy(rows_hbm.at[p_ref[j * TOP_K + k]],
                                          buf_ref.at[pl.ds(off, PACK_SLABS), :], sem.at[to_slot]).start()
            return carry

        lax.fori_loop(0, tm // COMBINE_UNROLL, body, 0)

    @pl.when(i == 0)
    def _():
        issue_tile(pos_ref, 0)

    @pl.when(i + 1 < pl.num_programs(0))
    def _():
        issue_tile(pos_next_ref, 1 - slot)

    base = pl.multiple_of(slot * n, PACK_SLABS)
    pltpu.make_async_copy(rows_flat_hbm.at[pl.ds(0, n)], buf_ref.at[pl.ds(base, n), :], sem.at[slot]).wait()

    gates = [gate_ref[:, k:k + 1] for k in range(TOP_K)]
    cols = []
    for s in range(PACK_SLABS):
        lo_acc = jnp.zeros((tm, LANES), F32)
        hi_acc = jnp.zeros((tm, LANES), F32)
        for k in range(TOP_K):
            lo, hi = _unpack_words(buf_ref[pl.ds(base + k * tm * PACK_SLABS + s, tm, stride=PACK_SLABS), :])
            lo_acc = lo_acc + gates[k] * lo
            hi_acc = hi_acc + gates[k] * hi
        cols.append(lo_acc)
        cols.append(hi_acc)
    ffn = jnp.concatenate(cols, axis=1)
    y = _layer_norm_rows(alpha * h_ref[...] + ffn, g_ref[...], be_ref[...])
    o_ref[...] = y
    obf_ref[...] = y.astype(BF16)


def _combine_ln(pos_flat, rows, gates, h, g, be, alpha):
    t, d = h.shape
    tm = min(COMBINE_TM, t)
    last = t // tm - 1
    row = lambda i: (i, 0)
    fixed = lambda i: (0, 0)
    return pl.pallas_call(
        functools.partial(_combine_ln_kernel, alpha),
        grid=(t // tm,),
        in_specs=[
            pl.BlockSpec((tm * TOP_K,), lambda i: (i,), memory_space=pltpu.SMEM),
            pl.BlockSpec((tm * TOP_K,), lambda i: (jnp.minimum(i + 1, last),), memory_space=pltpu.SMEM),
            pl.BlockSpec(memory_space=pl.ANY),
            pl.BlockSpec(memory_space=pl.ANY),
            pl.BlockSpec((tm, LANES), row),
            pl.BlockSpec((tm, d), row),
            pl.BlockSpec((1, d), fixed),
            pl.BlockSpec((1, d), fixed),
        ],
        out_specs=[pl.BlockSpec((tm, d), row), pl.BlockSpec((tm, d), row)],
        out_shape=[jax.ShapeDtypeStruct((t, d), F32), jax.ShapeDtypeStruct((t, d), BF16)],
        scratch_shapes=[pltpu.VMEM((2 * tm * TOP_K * PACK_SLABS, LANES), U32), pltpu.SemaphoreType.DMA((2,))],
        compiler_params=pltpu.CompilerParams(dimension_semantics=("arbitrary",)),
        name="combine_ln",
    )(pos_flat, pos_flat, rows, rows.reshape(-1, LANES), gates, h, g.reshape(1, d), be.reshape(1, d))


def _moe_layer(h, h_packed, router_w, router_b, w1, b1, w2, b2, layer, g, be, alpha):
    t, _ = h.shape
    n_e = router_w.shape[1]
    bm = MOE_BM
    n_blocks = -(-(t * TOP_K) // bm) + n_e
    idx_o, gate_o, rank_o, cnt_o = _router(h, router_w, router_b)
    idx = idx_o[:, :TOP_K]
    counts = cnt_o[0, :n_e].astype(I32)
    pcounts = (counts + bm - 1) // bm * bm
    pend = jnp.cumsum(pcounts)
    pstart = pend - pcounts
    hot = idx[:, :, None] == jnp.arange(n_e, dtype=I32)
    pos = (jnp.sum(jnp.where(hot, pstart, 0), axis=-1) + rank_o[:, :TOP_K].astype(I32)).reshape(-1)
    blk_row = jnp.arange(n_blocks, dtype=I32) * bm
    block_e = jnp.minimum(jnp.sum((pend[None, :] <= blk_row[:, None]).astype(I32), axis=1), n_e - 1)
    n_used = (pend[-1:] // bm).astype(I32)
    xs = _dispatch(h_packed.reshape(t, PACK_SLABS, LANES), pos, n_blocks * bm)
    act = _moe_up(block_e, n_used, xs.reshape(n_blocks * bm * PACK_SLABS, LANES), w1, b1, layer, n_blocks)
    rows = _moe_down(block_e, n_used, act, w2, b2, layer, n_blocks)
    return _combine_ln(pos, rows.reshape(n_blocks * bm, PACK_SLABS, LANES), gate_o, h, g, be, alpha)


def kernel(x, mem, positions, attn_w_in, attn_b_in, attn_sinks, ssd_w_in, ssd_b_in, ssd_conv_w, ssd_conv_b, ssd_dt_bias, ssd_a_log, ssd_d_skip, ssd_norm_g, conf_w_in, conf_b_in, conf_dw_w, conf_dw_b, conf_ln_g, conf_ln_b, mem_w_kv, w_out, b_out, ln1_g, ln1_b, router_w, router_b, moe_w1, moe_b1, moe_w2, moe_b2, ln2_g, ln2_b):
    bsz, seq, d = x.shape
    depth = w_out.shape[0]
    t = bsz * seq
    alpha = (2 * depth) ** 0.25
    n_mem = mem.shape[1]

    posf = positions.astype(F32).reshape(t, 1)
    inv_freq = ROPE_THETA ** (-jnp.arange(ROPE_HALF, dtype=F32) / ROPE_HALF)
    lane = jnp.arange(LANES) % HEAD_DIM
    freq_row = jnp.where(lane < ROPE_DIM, inv_freq[lane % ROPE_HALF], 0.0).reshape(1, LANES)
    mem2 = mem.reshape(bsz * n_mem, d)

    h = x.reshape(t, d)
    h_in = h
    for i in range(depth):
        kind, j = i % N_MIXERS, i // N_MIXERS
        if kind == 0:
            w_in, b_in = attn_w_in[j], attn_b_in[j]
            u = _linear(h_in, w_in[:, :-XA_WIDTH], b_in[:-XA_WIDTH], BF16)
            mix = _swa_mixer(u, posf, attn_sinks[j], freq_row, bsz, seq)
        elif kind == 1:
            w_in, b_in = ssd_w_in[j], ssd_b_in[j]
            z = _linear(h_in, w_in[:, :MIX_WIDTH], b_in[:MIX_WIDTH], BF16)
            xbc = _linear(h_in, w_in[:, MIX_WIDTH:MIX_WIDTH + SSD_CONV_CH],
                          b_in[MIX_WIDTH:MIX_WIDTH + SSD_CONV_CH], F32)
            dt_lo = MIX_WIDTH + SSD_CONV_CH
            pad = LANES - SSD_HEADS
            dt_raw = _linear(h_in, jnp.pad(w_in[:, dt_lo:dt_lo + SSD_HEADS], ((0, 0), (0, pad))),
                             jnp.pad(b_in[dt_lo:dt_lo + SSD_HEADS], (0, pad)), F32)
            mix = _ssd_mixer(z, xbc, dt_raw, ssd_conv_w[j], ssd_conv_b[j], ssd_dt_bias[j], ssd_a_log[j],
                             ssd_d_skip[j], ssd_norm_g[j], bsz, seq)
        else:
            w_in, b_in = conf_w_in[j], conf_b_in[j]
            u = _linear(h_in, w_in[:, :-XA_WIDTH], b_in[:-XA_WIDTH], BF16)
            mix = _conformer_mixer(u, conf_dw_w[j], conf_dw_b[j], conf_ln_g[j], conf_ln_b[j], bsz, seq)
        qm = _linear(h_in, w_in[:, -XA_WIDTH:], b_in[-XA_WIDTH:], BF16)
        kv = _linear(mem2, mem_w_kv[i], jnp.zeros((2 * XA_WIDTH,), F32), BF16)
        xa = _memory_attention(qm, kv, bsz, seq)
        h1, _, h1_packed = _outproj_ln(mix, xa, w_out[i], b_out[i], h, ln1_g[i], ln1_b[i], alpha)
        h, h_in = _moe_layer(h1, h1_packed, router_w[i], router_b[i], moe_w1, moe_b1, moe_w2, moe_b2, i,
                             ln2_g[i], ln2_b[i], alpha)
    return h.reshape(bsz, seq, d)
```

```python
import functools
import math

import jax
import jax.numpy as jnp
from jax import lax
from jax.experimental import pallas as pl
from jax.experimental.pallas import tpu as pltpu

F32 = jnp.float32
BF16 = jnp.bfloat16
U32 = jnp.uint32
I32 = jnp.int32

LANES = 128
SUBLANES = 8
PACK_SLABS = 8
PACK_SPAN = 2 * LANES

D_MODEL = 2048
N_MIXERS = 3
MIX_WIDTH = 1536
XA_HEADS = 4
XA_HEAD_DIM = 128
XA_WIDTH = XA_HEADS * XA_HEAD_DIM
HEAD_DIM = 64
N_Q_HEADS = MIX_WIDTH // HEAD_DIM
N_KV_HEADS = 3
Q_PER_KV = N_Q_HEADS // N_KV_HEADS
WINDOW = 128
ROPE_DIM = HEAD_DIM // 4
ROPE_HALF = ROPE_DIM // 2
ROPE_THETA = 500000.0
SSD_HEAD_DIM = 64
SSD_HEADS = MIX_WIDTH // SSD_HEAD_DIM
SSD_GROUPS = 4
SSD_HPG = SSD_HEADS // SSD_GROUPS
SSD_GROUP_WIDTH = SSD_HPG * SSD_HEAD_DIM
SSD_STATE = 128
SSD_CONV = 4
SSD_CHUNK = 128
SSD_BC = SSD_GROUPS * SSD_STATE
SSD_CONV_CH = MIX_WIDTH + 2 * SSD_BC
CONF_WIDTH = 31
CONF_HALO = 32
SSD_HALO = 8
TOP_K = 4
D_FF = 1024
SWIGLU_LIMIT = 7.0
SWIGLU_ALPHA = 1.702
LN_EPS = 1e-5

MOE_BM = 512
MOE_TF = 512
ROUTER_TM = 512
COMBINE_TM = 512
DISPATCH_TM = 512
DISPATCH_UNROLL = 8
COMBINE_UNROLL = 8
CONF_TR = 256
XA_TQ = 1024
OUT_TM = 512

HI_MASK = 0xFFFF0000

V7X_VMEM_BYTES = 64 * 1024 * 1024
VMEM_LIMIT_BYTES = V7X_VMEM_BYTES * 7 // 8


def _params(n_axes, **kw):
    return pltpu.CompilerParams(dimension_semantics=("arbitrary",) * n_axes,
                                vmem_limit_bytes=VMEM_LIMIT_BYTES, **kw)


def _sigmoid(x):
    return 1.0 / (1.0 + jnp.exp(-x))


def _silu(x):
    return x * _sigmoid(x)


def _pick_tile(n, cap):
    best = LANES
    for t in range(LANES, cap + 1, LANES):
        if n % t == 0:
            best = t
    return best


def _pack_words(a, b):
    a = a.astype(BF16).astype(F32)
    b = b.astype(BF16).astype(F32)
    return (pltpu.bitcast(a, U32) >> 16) | (pltpu.bitcast(b, U32) & jnp.uint32(HI_MASK))


def _unpack_words(w):
    lo = pltpu.bitcast(w << 16, F32)
    hi = pltpu.bitcast(w & jnp.uint32(HI_MASK), F32)
    return lo, hi


def _layer_norm_rows(y, g, b):
    mu = jnp.mean(y, axis=-1, keepdims=True)
    d = y - mu
    var = jnp.mean(d * d, axis=-1, keepdims=True)
    return d * lax.rsqrt(var + LN_EPS) * g + b


def _linear_kernel(x_ref, w_ref, b_ref, o_ref):
    x = x_ref[...].astype(BF16)
    acc = jnp.dot(x, w_ref[...], preferred_element_type=F32) + b_ref[...]
    o_ref[...] = acc.astype(o_ref.dtype)


def _linear(x, w, b, out_dtype, tm=1024, tn_cap=1024):
    m, k = x.shape
    n = w.shape[1]
    tn = _pick_tile(n, tn_cap)
    tm = min(tm, m)
    return pl.pallas_call(
        _linear_kernel,
        grid=(n // tn, m // tm),
        in_specs=[
            pl.BlockSpec((tm, k), lambda j, i: (i, 0)),
            pl.BlockSpec((k, tn), lambda j, i: (0, j)),
            pl.BlockSpec((1, tn), lambda j, i: (0, j)),
        ],
        out_specs=pl.BlockSpec((tm, tn), lambda j, i: (i, j)),
        out_shape=jax.ShapeDtypeStruct((m, n), out_dtype),
        compiler_params=_params(2),
        name="linear",
    )(x, w.astype(BF16), b.reshape(1, n).astype(F32))


def _outproj_ln_kernel(alpha, mix_ref, xa_ref, wm_ref, wx_ref, b_ref, h_ref, g_ref, be_ref,
                       o_ref, obf_ref, opk_ref):
    sub = jnp.dot(mix_ref[...], wm_ref[...], preferred_element_type=F32)
    sub = sub + jnp.dot(xa_ref[...], wx_ref[...], preferred_element_type=F32) + b_ref[...]
    y = _layer_norm_rows(alpha * h_ref[...] + sub, g_ref[...], be_ref[...])
    o_ref[...] = y
    obf_ref[...] = y.astype(BF16)
    tm = y.shape[0]
    for s in range(PACK_SLABS):
        lo = y[:, PACK_SPAN * s:PACK_SPAN * s + LANES]
        hi = y[:, PACK_SPAN * s + LANES:PACK_SPAN * (s + 1)]
        opk_ref[pl.ds(s, tm, stride=PACK_SLABS), :] = _pack_words(lo, hi)


def _outproj_ln(mix, xa, w_out, b_out, h, g, be, alpha):
    t, d = h.shape
    tm = min(OUT_TM, t)
    wm = w_out[:MIX_WIDTH].astype(BF16)
    wx = w_out[MIX_WIDTH:].astype(BF16)
    row = lambda i: (i, 0)
    fixed = lambda i: (0, 0)
    return pl.pallas_call(
        functools.partial(_outproj_ln_kernel, alpha),
        grid=(t // tm,),
        in_specs=[
            pl.BlockSpec((tm, MIX_WIDTH), row),
            pl.BlockSpec((tm, XA_WIDTH), row),
            pl.BlockSpec((MIX_WIDTH, d), fixed),
            pl.BlockSpec((XA_WIDTH, d), fixed),
            pl.BlockSpec((1, d), fixed),
            pl.BlockSpec((tm, d), row),
            pl.BlockSpec((1, d), fixed),
            pl.BlockSpec((1, d), fixed),
        ],
        out_specs=[
            pl.BlockSpec((tm, d), row),
            pl.BlockSpec((tm, d), row),
            pl.BlockSpec((tm * PACK_SLABS, LANES), row),
        ],
        out_shape=[
            jax.ShapeDtypeStruct((t, d), F32),
            jax.ShapeDtypeStruct((t, d), BF16),
            jax.ShapeDtypeStruct((t * PACK_SLABS, LANES), U32),
        ],
        compiler_params=_params(1),
        name="outproj_ln",
    )(mix, xa, wm, wx, b_out.reshape(1, d), h, g.reshape(1, d), be.reshape(1, d))


def _xa_kernel(q_ref, k_ref, v_ref, o_ref):
    scale = XA_HEAD_DIM ** -0.5
    outs = []
    for hd in range(XA_HEADS):
        sl = slice(hd * XA_HEAD_DIM, (hd + 1) * XA_HEAD_DIM)
        s = lax.dot_general(q_ref[:, sl], k_ref[:, sl], (((1,), (1,)), ((), ())),
                            preferred_element_type=F32) * scale
        m = jnp.max(s, axis=-1, keepdims=True)
        p = jnp.exp(s - m)
        den = jnp.sum(p, axis=-1, keepdims=True)
        o = jnp.dot(p.astype(BF16), v_ref[:, sl], preferred_element_type=F32)
        outs.append(o / den)
    o_ref[...] = jnp.concatenate(outs, axis=-1).astype(o_ref.dtype)


def _memory_attention(qm, kv, bsz, seq):
    n_mem = kv.shape[0] // bsz
    tq = min(XA_TQ, seq)
    nq = seq // tq
    return pl.pallas_call(
        _xa_kernel,
        grid=(bsz, nq),
        in_specs=[
            pl.BlockSpec((tq, XA_WIDTH), lambda b, i: (b * nq + i, 0)),
            pl.BlockSpec((n_mem, XA_WIDTH), lambda b, i: (b, 0)),
            pl.BlockSpec((n_mem, XA_WIDTH), lambda b, i: (b, 1)),
        ],
        out_specs=pl.BlockSpec((tq, XA_WIDTH), lambda b, i: (b * nq + i, 0)),
        out_shape=jax.ShapeDtypeStruct((bsz * seq, XA_WIDTH), BF16),
        compiler_params=_params(2),
        name="memory_attention",
    )(qm, kv, kv)


def _rope_tables(pos, freq_row):
    ang = pos * freq_row
    c = jnp.cos(ang)
    s = jnp.sin(ang)
    d = lax.broadcasted_iota(I32, ang.shape, 1) % HEAD_DIM
    cmul = jnp.where(d < ROPE_DIM, c, 1.0)
    s_up = jnp.where(d < ROPE_HALF, -s, 0.0)
    s_dn = jnp.where((d >= ROPE_HALF) & (d < ROPE_DIM), s, 0.0)
    return cmul, s_up, s_dn


def _rope_chunk(t, tabs):
    cmul, s_up, s_dn = tabs
    up = pltpu.roll(t, LANES - ROPE_HALF, axis=1)
    dn = pltpu.roll(t, ROPE_HALF, axis=1)
    return t * cmul + up * s_up + dn * s_dn


def _swa_kernel(sinks_ref, q_ref, kvc_ref, posc_ref, freq_ref, o_ref, kv_win):
    n = pl.program_id(1)
    w = WINDOW
    tab_c = _rope_tables(posc_ref[...], freq_ref[...])
    lane = lax.broadcasted_iota(I32, (w, LANES), 1)
    first_half = lane < HEAD_DIM

    slot = n % 2

    @pl.when(n == 0)
    def _():
        kv_win[1] = jnp.zeros(kv_win.shape[1:], BF16)

    c0 = _rope_chunk(kvc_ref[:, 0:LANES].astype(F32), tab_c)
    c1 = kvc_ref[:, LANES:2 * LANES].astype(F32)
    c1 = jnp.where(first_half, _rope_chunk(c1, tab_c), c1)
    c2 = kvc_ref[:, 2 * LANES:3 * LANES].astype(F32)
    kv_cur = jnp.concatenate([c0, c1, c2], axis=1).astype(BF16)
    kv_win[slot] = kv_cur
    kv = jnp.concatenate([kv_win[1 - slot], kv_cur], axis=0)

    scale = HEAD_DIM ** -0.5
    qs = []
    for j in range(MIX_WIDTH // LANES):
        qj = _rope_chunk(q_ref[:, j * LANES:(j + 1) * LANES].astype(F32), tab_c) * scale
        qs.append(qj.astype(BF16))

    rows = Q_PER_KV * w
    qi = lax.broadcasted_iota(I32, (rows, 2 * w), 0) % w
    sj = lax.broadcasted_iota(I32, (rows, 2 * w), 1)
    rel = qi + w - sj
    mask = (rel >= 0) & (rel < w) & ((n > 0) | (sj >= w))

    outs = []
    for g in range(N_KV_HEADS):
        k_g = kv[:, g * HEAD_DIM:(g + 1) * HEAD_DIM]
        v_g = kv[:, (N_KV_HEADS + g) * HEAD_DIM:(N_KV_HEADS + g + 1) * HEAD_DIM]
        q_parts = []
        sink_parts = []
        for r in range(Q_PER_KV):
            hd = g * Q_PER_KV + r
            chunk = qs[hd // 2]
            q_parts.append(chunk[:, (hd % 2) * HEAD_DIM:(hd % 2 + 1) * HEAD_DIM])
            sink_parts.append(jnp.full((w, 1), sinks_ref[hd], F32))
        q_g = jnp.concatenate(q_parts, axis=0)
        sink = jnp.concatenate(sink_parts, axis=0)
        s = lax.dot_general(q_g, k_g, (((1,), (1,)), ((), ())), preferred_element_type=F32)
        s = jnp.where(mask, s, -jnp.inf)
        m = jnp.maximum(jnp.max(s, axis=-1, keepdims=True), sink)
        p = jnp.exp(s - m)
        den = jnp.sum(p, axis=-1, keepdims=True) + jnp.exp(sink - m)
        o = jnp.dot(p.astype(BF16), v_g, preferred_element_type=F32) / den
        for r in range(Q_PER_KV):
            outs.append(o[r * w:(r + 1) * w, :])
    o_ref[...] = jnp.concatenate(outs, axis=1).astype(o_ref.dtype)


def _swa_mixer(u, posf, sinks, freq_row, bsz, seq):
    nb = seq // WINDOW
    kv_w = 2 * N_KV_HEADS * HEAD_DIM
    kv_col = MIX_WIDTH // kv_w
    cur = lambda b, n, s: (b * nb + n, 0)
    return pl.pallas_call(
        _swa_kernel,
        grid_spec=pltpu.PrefetchScalarGridSpec(
            num_scalar_prefetch=1,
            grid=(bsz, nb),
            in_specs=[
                pl.BlockSpec((WINDOW, MIX_WIDTH), cur),
                pl.BlockSpec((WINDOW, kv_w), lambda b, n, s: (b * nb + n, kv_col)),
                pl.BlockSpec((WINDOW, 1), cur),
                pl.BlockSpec((1, LANES), lambda b, n, s: (0, 0)),
            ],
            out_specs=pl.BlockSpec((WINDOW, MIX_WIDTH), cur),
            scratch_shapes=[pltpu.VMEM((2, WINDOW, kv_w), BF16)],
        ),
        out_shape=jax.ShapeDtypeStruct((bsz * seq, MIX_WIDTH), BF16),
        compiler_params=_params(2),
        name="swa",
    )(sinks.astype(F32), u, u, posf, freq_row)


def _split3(x):
    p0 = x.astype(BF16)
    r = x - p0.astype(F32)
    p1 = r.astype(BF16)
    p2 = (r - p1.astype(F32)).astype(BF16)
    return p0, p1, p2


def _expand_heads(arr, g):
    rows = arr.shape[0]
    parts = [jnp.broadcast_to(arr[:, g * SSD_HPG + r:g * SSD_HPG + r + 1], (rows, SSD_HEAD_DIM))
             for r in range(SSD_HPG)]
    return jnp.concatenate(parts, axis=1)


def _ssd_kernel(z_ref, xbc_ref, dt_ref, cw_ref, cb_ref, dtb_ref, alog_ref, dskip_ref, ng_ref,
                o_ref, ext_ref, state_ref):
    c = pl.program_id(1)
    q = SSD_CHUNK

    @pl.when(c == 0)
    def _():
        ext_ref[0:SSD_HALO, :] = jnp.zeros((SSD_HALO, SSD_CONV_CH), F32)
        state_ref[...] = jnp.zeros(state_ref.shape, F32)

    ext_ref[SSD_HALO:SSD_HALO + q, :] = xbc_ref[...].astype(F32)
    conv = cb_ref[...]
    for k in range(SSD_CONV):
        conv = conv + cw_ref[k:k + 1, :] * ext_ref[pl.ds(SSD_HALO - (SSD_CONV - 1) + k, q), :]
    ext_ref[0:SSD_HALO, :] = ext_ref[q:q + SSD_HALO, :]
    xbc = _silu(conv)

    lane = lax.broadcasted_iota(I32, (q, LANES), 1)
    x = dt_ref[...] + dtb_ref[...]
    dt = jnp.maximum(x, 0.0) + jnp.log1p(jnp.exp(-jnp.abs(x)))
    a = jnp.where(lane < SSD_HEADS, dt * (-jnp.exp(alog_ref[...])), 0.0)

    row = lax.broadcasted_iota(I32, (q, q), 0)
    col = lax.broadcasted_iota(I32, (q, q), 1)
    causal = row >= col
    tri = causal.astype(BF16)
    a_cs = jnp.zeros((q, LANES), F32)
    for piece in _split3(a):
        a_cs = a_cs + jnp.dot(tri, piece, preferred_element_type=F32)
    a_cs_t = a_cs.T
    a_last = jnp.broadcast_to(a_cs[q - 1:q, :], (q, LANES))

    ys = []
    for g in range(SSD_GROUPS):
        xs_g = xbc[:, g * SSD_GROUP_WIDTH:(g + 1) * SSD_GROUP_WIDTH]
        b_g = xbc[:, MIX_WIDTH + g * SSD_STATE:MIX_WIDTH + (g + 1) * SSD_STATE].astype(BF16)
        c_g = xbc[:, MIX_WIDTH + SSD_BC + g * SSD_STATE:MIX_WIDTH + SSD_BC + (g + 1) * SSD_STATE].astype(BF16)
        cb = lax.dot_general(c_g, b_g, (((1,), (1,)), ((), ())), preferred_element_type=F32)
        acs_e = _expand_heads(a_cs, g)
        last_e = _expand_heads(a_last, g)
        xdt = xs_g * _expand_heads(dt, g)
        xdt_bf = xdt.astype(BF16)

        diag = []
        for r in range(SSD_HPG):
            hd = g * SSD_HPG + r
            seg = a_cs[:, hd:hd + 1] - a_cs_t[hd:hd + 1, :]
            decay = jnp.exp(jnp.where(causal, seg, -jnp.inf))
            mat = (cb * decay).astype(BF16)
            diag.append(jnp.dot(mat, xdt_bf[:, r * SSD_HEAD_DIM:(r + 1) * SSD_HEAD_DIM],
                                preferred_element_type=F32))
        y_diag = jnp.concatenate(diag, axis=1)

        prev = state_ref[g]
        y_off = jnp.dot(c_g, prev.astype(BF16), preferred_element_type=F32) * jnp.exp(acs_e)
        wgt = (xdt * jnp.exp(last_e - acs_e)).astype(BF16)
        upd = lax.dot_general(b_g, wgt, (((0,), (0,)), ((), ())), preferred_element_type=F32)
        state_ref[g] = prev * jnp.exp(last_e[0:1, :]) + upd
        ys.append(y_diag + y_off + dskip_ref[:, g * SSD_GROUP_WIDTH:(g + 1) * SSD_GROUP_WIDTH] * xs_g)

    y = jnp.concatenate(ys, axis=1) * _silu(z_ref[...].astype(F32))
    ms = jnp.mean(y * y, axis=-1, keepdims=True)
    o_ref[...] = (y * lax.rsqrt(ms + LN_EPS) * ng_ref[...]).astype(o_ref.dtype)


def _ssd_mixer(z, xbc, dt_raw, conv_w, conv_b, dt_bias, a_log, d_skip, norm_g, bsz, seq):
    nc = seq // SSD_CHUNK
    pad = LANES - SSD_HEADS
    row = lambda b, c: (b * nc + c, 0)
    fixed = lambda b, c: (0, 0)
    return pl.pallas_call(
        _ssd_kernel,
        grid=(bsz, nc),
        in_specs=[
            pl.BlockSpec((SSD_CHUNK, MIX_WIDTH), row),
            pl.BlockSpec((SSD_CHUNK, SSD_CONV_CH), row),
            pl.BlockSpec((SSD_CHUNK, LANES), row),
            pl.BlockSpec((SSD_CONV, SSD_CONV_CH), fixed),
            pl.BlockSpec((1, SSD_CONV_CH), fixed),
            pl.BlockSpec((1, LANES), fixed),
            pl.BlockSpec((1, LANES), fixed),
            pl.BlockSpec((1, MIX_WIDTH), fixed),
            pl.BlockSpec((1, MIX_WIDTH), fixed),
        ],
        out_specs=pl.BlockSpec((SSD_CHUNK, MIX_WIDTH), row),
        out_shape=jax.ShapeDtypeStruct((bsz * seq, MIX_WIDTH), BF16),
        scratch_shapes=[
            pltpu.VMEM((SSD_HALO + SSD_CHUNK, SSD_CONV_CH), F32),
            pltpu.VMEM((SSD_GROUPS, SSD_STATE, SSD_GROUP_WIDTH), F32),
        ],
        compiler_params=_params(2),
        name="ssd",
    )(z, xbc, dt_raw, conv_w, conv_b.reshape(1, -1),
      jnp.pad(dt_bias, (0, pad)).reshape(1, LANES), jnp.pad(a_log, (0, pad)).reshape(1, LANES),
      jnp.repeat(d_skip, SSD_HEAD_DIM).reshape(1, MIX_WIDTH), norm_g.reshape(1, MIX_WIDTH))


def _conformer_kernel(a_ref, b_ref, w_ref, wb_ref, g_ref, be_ref, o_ref, ext_ref, acc_ref, shift_ref):
    i = pl.program_id(1)
    tr = a_ref.shape[0]

    @pl.when(i == 0)
    def _():
        ext_ref[0:CONF_HALO, :] = jnp.zeros((CONF_HALO, MIX_WIDTH), F32)

    ext_ref[CONF_HALO:CONF_HALO + tr, :] = a_ref[...].astype(F32) * _sigmoid(b_ref[...].astype(F32))
    first = CONF_HALO - (CONF_WIDTH - 1)
    span = shift_ref.shape[1]
    for j in range(MIX_WIDTH // LANES):
        cs = slice(j * LANES, (j + 1) * LANES)
        for a in range(1, SUBLANES):
            shift_ref[a - 1] = ext_ref[pl.ds(a, span), cs]
        acc = jnp.broadcast_to(wb_ref[:, cs], (tr, LANES))
        for k in range(CONF_WIDTH):
            a = (first + k) % SUBLANES
            base = first + k - a
            tap = ext_ref[pl.ds(base, tr), cs] if a == 0 else shift_ref[a - 1, pl.ds(base, tr), :]
            acc = acc + w_ref[k:k + 1, cs] * tap
        acc_ref[:, cs] = acc
    ext_ref[0:CONF_HALO, :] = ext_ref[tr:tr + CONF_HALO, :]
    y = _layer_norm_rows(acc_ref[...], g_ref[...], be_ref[...])
    o_ref[...] = _silu(y).astype(o_ref.dtype)


def _conformer_mixer(u, dw_w, dw_b, ln_g, ln_b, bsz, seq):
    tr = min(CONF_TR, seq)
    nr = seq // tr
    fixed = lambda b, i: (0, 0)
    return pl.pallas_call(
        _conformer_kernel,
        grid=(bsz, nr),
        in_specs=[
            pl.BlockSpec((tr, MIX_WIDTH), lambda b, i: (b * nr + i, 0)),
            pl.BlockSpec((tr, MIX_WIDTH), lambda b, i: (b * nr + i, 1)),
            pl.BlockSpec((CONF_WIDTH, MIX_WIDTH), fixed),
            pl.BlockSpec((1, MIX_WIDTH), fixed),
            pl.BlockSpec((1, MIX_WIDTH), fixed),
            pl.BlockSpec((1, MIX_WIDTH), fixed),
        ],
        out_specs=pl.BlockSpec((tr, MIX_WIDTH), lambda b, i: (b * nr + i, 0)),
        out_shape=jax.ShapeDtypeStruct((bsz * seq, MIX_WIDTH), BF16),
        scratch_shapes=[
            pltpu.VMEM((CONF_HALO + tr, MIX_WIDTH), F32),
            pltpu.VMEM((tr, MIX_WIDTH), F32),
            pltpu.VMEM((SUBLANES - 1, tr + CONF_HALO - SUBLANES, LANES), F32),
        ],
        compiler_params=_params(2),
        name="conformer",
    )(u, u, dw_w, dw_b.reshape(1, -1), ln_g.reshape(1, -1), ln_b.reshape(1, -1))


def _router_kernel(n_experts, h_ref, w_ref, wlo_ref, b_ref, idx_ref, gate_ref, rank_ref, cnt_ref, run_ref):
    i = pl.program_id(0)
    tm = h_ref.shape[0]

    @pl.when(i == 0)
    def _():
        run_ref[...] = jnp.zeros(run_ref.shape, F32)

    x = h_ref[...]
    x_hi = x.astype(BF16)
    x_lo = (x - x_hi.astype(F32)).astype(BF16)
    logits = (jnp.dot(x_hi, w_ref[...], preferred_element_type=F32)
              + jnp.dot(x_lo, w_ref[...], preferred_element_type=F32)
              + jnp.dot(x_hi, wlo_ref[...], preferred_element_type=F32)) + b_ref[...]
    lane = lax.broadcasted_iota(I32, (tm, LANES), 1).astype(F32)
    cur = jnp.where(lane < n_experts, logits, -jnp.inf)
    vals, idxs, hots = [], [], []
    for _ in range(TOP_K):
        m = jnp.max(cur, axis=-1, keepdims=True)
        ix = jnp.min(jnp.where(cur == m, lane, float(LANES)), axis=-1, keepdims=True)
        hot = lane == ix
        vals.append(m)
        idxs.append(ix)
        hots.append(hot)
        cur = jnp.where(hot, -jnp.inf, cur)
    exps = [jnp.exp(v - vals[0]) for v in vals]
    den = exps[0]
    for e in exps[1:]:
        den = den + e

    r = lax.broadcasted_iota(I32, (tm, tm), 0)
    c = lax.broadcasted_iota(I32, (tm, tm), 1)
    before = (r > c).astype(BF16)
    base = run_ref[0:1, :]
    ranks = []
    for hot in hots:
        hot_f = hot.astype(F32)
        earlier = jnp.dot(before, hot.astype(BF16), preferred_element_type=F32)
        ranks.append(jnp.sum(hot_f * (earlier + base), axis=-1, keepdims=True))
        base = base + jnp.sum(hot_f, axis=0, keepdims=True)
    run_ref[0:1, :] = base
    cnt_ref[...] = jnp.broadcast_to(base, cnt_ref.shape)

    idx_out = jnp.zeros((tm, LANES), F32)
    gate_out = jnp.zeros((tm, LANES), F32)
    rank_out = jnp.zeros((tm, LANES), F32)
    for k in range(TOP_K):
        sel = lane == k
        idx_out = jnp.where(sel, idxs[k], idx_out)
        gate_out = jnp.where(sel, exps[k] / den, gate_out)
        rank_out = jnp.where(sel, ranks[k], rank_out)
    idx_ref[...] = idx_out.astype(I32)
    gate_ref[...] = gate_out
    rank_ref[...] = rank_out


def _router(h, router_w, router_b):
    t, d = h.shape
    n_experts = router_w.shape[1]
    tm = min(ROUTER_TM, t)
    w = jnp.pad(router_w.astype(F32), ((0, 0), (0, LANES - n_experts)))
    w_hi = w.astype(BF16)
    w_lo = (w - w_hi.astype(F32)).astype(BF16)
    b = jnp.pad(router_b.astype(F32), (0, LANES - n_experts)).reshape(1, LANES)
    row = lambda i: (i, 0)
    fixed = lambda i: (0, 0)
    return pl.pallas_call(
        functools.partial(_router_kernel, n_experts),
        grid=(t // tm,),
        in_specs=[pl.BlockSpec((tm, d), row), pl.BlockSpec((d, LANES), fixed), pl.BlockSpec((d, LANES), fixed),
                  pl.BlockSpec((1, LANES), fixed)],
        out_specs=[pl.BlockSpec((tm, LANES), row), pl.BlockSpec((tm, LANES), row),
                   pl.BlockSpec((tm, LANES), row), pl.BlockSpec((8, LANES), fixed)],
        out_shape=[jax.ShapeDtypeStruct((t, LANES), I32), jax.ShapeDtypeStruct((t, LANES), F32),
                   jax.ShapeDtypeStruct((t, LANES), F32), jax.ShapeDtypeStruct((8, LANES), F32)],
        scratch_shapes=[pltpu.VMEM((8, LANES), F32)],
        compiler_params=_params(1),
        name="router",
    )(h, w_hi, w_lo, b)


def _dispatch_kernel(pos_ref, src_ref, dst_in, dst_hbm, sem):
    del dst_in
    tm = src_ref.shape[0]

    def issue(jo, carry):
        for ji in range(DISPATCH_UNROLL):
            j = jo * DISPATCH_UNROLL + ji
            for k in range(TOP_K):
                pltpu.make_async_copy(src_ref.at[j], dst_hbm.at[pos_ref[j * TOP_K + k]],
                                      sem).start(priority=k % 2)
        return carry

    lax.fori_loop(0, tm // DISPATCH_UNROLL, issue, 0)
    n = tm * TOP_K
    pltpu.make_async_copy(dst_hbm.at[pl.ds(0, n)], dst_hbm.at[pl.ds(0, n)], sem).wait()


def _dispatch(h_packed, pos_flat, n_rows):
    t = h_packed.shape[0]
    tm = min(DISPATCH_TM, t)
    zeros = jnp.zeros((n_rows, PACK_SLABS, LANES), U32)
    return pl.pallas_call(
        _dispatch_kernel,
        grid=(t // tm,),
        in_specs=[
            pl.BlockSpec((tm * TOP_K,), lambda i: (i,), memory_space=pltpu.SMEM),
            pl.BlockSpec((tm, PACK_SLABS, LANES), lambda i: (i, 0, 0)),
            pl.BlockSpec(memory_space=pl.ANY),
        ],
        out_specs=pl.BlockSpec(memory_space=pl.ANY),
        out_shape=jax.ShapeDtypeStruct((n_rows, PACK_SLABS, LANES), U32),
        scratch_shapes=[pltpu.SemaphoreType.DMA(())],
        input_output_aliases={2: 0},
        compiler_params=_params(1, has_side_effects=True),
        name="dispatch",
    )(pos_flat, h_packed, zeros)


def _row_block_pipeline(first, cnt, in_copy, out_copy, compute):
    @pl.when(cnt > 0)
    def _():
        in_copy(first, 0).start()

    def body(r, carry):
        slot = r % 2
        in_copy(first + r, slot).wait()

        @pl.when(r + 1 < cnt)
        def _():
            in_copy(first + r + 1, 1 - slot).start()

        @pl.when(r >= 2)
        def _():
            out_copy(first + r - 2, slot).wait()

        compute(slot)
        out_copy(first + r, slot).start()
        return carry

    lax.fori_loop(0, cnt, body, 0)

    @pl.when(cnt >= 2)
    def _():
        out_copy(first + cnt - 2, cnt % 2).wait()

    @pl.when(cnt >= 1)
    def _():
        out_copy(first + cnt - 1, (cnt + 1) % 2).wait()


def _zero_unused_blocks(used, n_blocks, o_buf, out_copy):
    o_buf[0] = jnp.zeros(o_buf.shape[1:], o_buf.dtype)

    def start(blk, carry):
        out_copy(blk, 0).start()
        return carry

    def wait(blk, carry):
        out_copy(blk, 0).wait()
        return carry

    lax.fori_loop(used, n_blocks, start, 0)
    lax.fori_loop(used, n_blocks, wait, 0)


def _moe_up_kernel(first_ref, cnt_ref, x_hbm, wg_ref, wl_ref, bg_ref, bl_ref, o_hbm,
                   wg_bf, wl_bf, x_buf, x_bf, o_buf, in_sem, out_sem):
    n = pl.program_id(0)
    e = pl.program_id(1)
    bm, tf = o_buf.shape[1], o_buf.shape[2]
    rows_in = bm * PACK_SLABS
    first, cnt = first_ref[e], cnt_ref[e]

    @pl.when(cnt > 0)
    def _():
        wg_bf[...] = wg_ref[0].astype(BF16)
        wl_bf[...] = wl_ref[0].astype(BF16)

    def in_copy(blk, slot):
        src = x_hbm.at[pl.ds(pl.multiple_of(blk * rows_in, rows_in), rows_in), :]
        return pltpu.make_async_copy(src, x_buf.at[slot], in_sem.at[slot])

    def out_copy(blk, slot):
        dst = o_hbm.at[pl.ds(pl.multiple_of(blk * bm, bm), bm), pl.ds(pl.multiple_of(n * tf, tf), tf)]
        return pltpu.make_async_copy(o_buf.at[slot], dst, out_sem.at[slot])

    def compute(slot):
        for s in range(PACK_SLABS):
            lo, hi = _unpack_words(x_buf[slot, pl.ds(s, bm, stride=PACK_SLABS), :])
            x_bf[:, PACK_SPAN * s:PACK_SPAN * s + LANES] = lo.astype(BF16)
            x_bf[:, PACK_SPAN * s + LANES:PACK_SPAN * (s + 1)] = hi.astype(BF16)
        x = x_bf[...]
        glu = jnp.dot(x, wg_bf[...], preferred_element_type=F32) + bg_ref[0]
        lin = jnp.dot(x, wl_bf[...], preferred_element_type=F32) + bl_ref[0]
        glu = jnp.minimum(glu, SWIGLU_LIMIT)
        lin = jnp.clip(lin, -SWIGLU_LIMIT, SWIGLU_LIMIT)
        o_buf[slot] = (glu * _sigmoid(SWIGLU_ALPHA * glu) * (lin + 1.0)).astype(o_buf.dtype)

    _row_block_pipeline(first, cnt, in_copy, out_copy, compute)

    @pl.when(e == pl.num_programs(1) - 1)
    def _():
        _zero_unused_blocks(first + cnt, o_hbm.shape[0] // bm, o_buf, out_copy)


def _moe_up(blk_first, blk_cnt, xs_flat, w1, b1, layer, n_blocks):
    n_layers, n_e, d, two_ff = w1.shape
    d_ff = two_ff // 2
    tf = MOE_TF
    nt = d_ff // tf
    bm = MOE_BM
    b1r = b1.reshape(n_layers, n_e, 1, two_ff)
    return pl.pallas_call(
        _moe_up_kernel,
        grid_spec=pltpu.PrefetchScalarGridSpec(
            num_scalar_prefetch=2,
            grid=(nt, n_e),
            in_specs=[
                pl.BlockSpec(memory_space=pl.ANY),
                pl.BlockSpec((None, 1, d, tf), lambda n, e, bf, bc: (layer, e, 0, n)),
                pl.BlockSpec((None, 1, d, tf), lambda n, e, bf, bc: (layer, e, 0, n + nt)),
                pl.BlockSpec((None, 1, 1, tf), lambda n, e, bf, bc: (layer, e, 0, n)),
                pl.BlockSpec((None, 1, 1, tf), lambda n, e, bf, bc: (layer, e, 0, n + nt)),
            ],
            out_specs=pl.BlockSpec(memory_space=pl.ANY),
            scratch_shapes=[
                pltpu.VMEM((d, tf), BF16), pltpu.VMEM((d, tf), BF16),
                pltpu.VMEM((2, bm * PACK_SLABS, LANES), U32), pltpu.VMEM((bm, d), BF16),
                pltpu.VMEM((2, bm, tf), BF16),
                pltpu.SemaphoreType.DMA((2,)), pltpu.SemaphoreType.DMA((2,)),
            ],
        ),
        out_shape=jax.ShapeDtypeStruct((n_blocks * bm, d_ff), BF16),
        compiler_params=_params(2, has_side_effects=True),
        name="moe_up",
    )(blk_first, blk_cnt, xs_flat, w1, w1, b1r, b1r)


def _moe_down_kernel(first_ref, cnt_ref, a_hbm, w_ref, b_ref, o_hbm, w_bf, a_buf, o_buf, in_sem, out_sem):
    e = pl.program_id(0)
    bm = a_buf.shape[1]
    rows_out = bm * PACK_SLABS
    first, cnt = first_ref[e], cnt_ref[e]

    @pl.when(cnt > 0)
    def _():
        w_bf[...] = w_ref[0].astype(BF16)

    def in_copy(blk, slot):
        src = a_hbm.at[pl.ds(pl.multiple_of(blk * bm, bm), bm), :]
        return pltpu.make_async_copy(src, a_buf.at[slot], in_sem.at[slot])

    def out_copy(blk, slot):
        dst = o_hbm.at[pl.ds(pl.multiple_of(blk * rows_out, rows_out), rows_out), :]
        return pltpu.make_async_copy(o_buf.at[slot], dst, out_sem.at[slot])

    def compute(slot):
        y = jnp.dot(a_buf[slot], w_bf[...], preferred_element_type=F32) + b_ref[0]
        for s in range(PACK_SLABS):
            lo = y[:, PACK_SPAN * s:PACK_SPAN * s + LANES]
            hi = y[:, PACK_SPAN * s + LANES:PACK_SPAN * (s + 1)]
            o_buf[slot, pl.ds(s, bm, stride=PACK_SLABS), :] = _pack_words(lo, hi)

    _row_block_pipeline(first, cnt, in_copy, out_copy, compute)

    @pl.when(e == pl.num_programs(0) - 1)
    def _():
        _zero_unused_blocks(first + cnt, o_hbm.shape[0] // rows_out, o_buf, out_copy)


def _moe_down(blk_first, blk_cnt, act, w2, b2, layer, n_blocks):
    n_layers, n_e, d_ff, d = w2.shape
    bm = MOE_BM
    return pl.pallas_call(
        _moe_down_kernel,
        grid_spec=pltpu.PrefetchScalarGridSpec(
            num_scalar_prefetch=2,
            grid=(n_e,),
            in_specs=[
                pl.BlockSpec(memory_space=pl.ANY),
                pl.BlockSpec((None, 1, d_ff, d), lambda e, bf, bc: (layer, e, 0, 0)),
                pl.BlockSpec((None, 1, 1, d), lambda e, bf, bc: (layer, e, 0, 0)),
            ],
            out_specs=pl.BlockSpec(memory_space=pl.ANY),
            scratch_shapes=[
                pltpu.VMEM((d_ff, d), BF16),
                pltpu.VMEM((2, bm, d_ff), BF16),
                pltpu.VMEM((2, bm * PACK_SLABS, LANES), U32),
                pltpu.SemaphoreType.DMA((2,)), pltpu.SemaphoreType.DMA((2,)),
            ],
        ),
        out_shape=jax.ShapeDtypeStruct((n_blocks * bm * PACK_SLABS, LANES), U32),
        compiler_params=_params(1, has_side_effects=True),
        name="moe_down",
    )(blk_first, blk_cnt, act, w2, b2.reshape(n_layers, n_e, 1, d))


def _combine_ln_kernel(alpha, pos_ref, pos_next_ref, rows_hbm, rows_flat_hbm, gate_ref, h_ref, g_ref,
                       be_ref, o_ref, obf_ref, buf_ref, sem):
    i = pl.program_id(0)
    tm = h_ref.shape[0]
    n = tm * TOP_K * PACK_SLABS
    slot = i % 2

    def issue_tile(p_ref, to_slot):
        base = to_slot * n

        def body(jo, carry):
            for ji in range(COMBINE_UNROLL):
                j = jo * COMBINE_UNROLL + ji
                for k in range(TOP_K):
                    off = pl.multiple_of(base + (k * tm + j) * PACK_SLABS, PACK_SLABS)
                    pltpu.make_async_copy(rows_hbm.at[p_ref[j * TOP_K + k]],
                                          buf_ref.at[pl.ds(off, PACK_SLABS), :],
                                          sem.at[to_slot]).start(priority=k % 2)
            return carry

        lax.fori_loop(0, tm // COMBINE_UNROLL, body, 0)

    @pl.when(i == 0)
    def _():
        issue_tile(pos_ref, 0)

    @pl.when(i + 1 < pl.num_programs(0))
    def _():
        issue_tile(pos_next_ref, 1 - slot)

    base = pl.multiple_of(slot * n, PACK_SLABS)
    pltpu.make_async_copy(rows_flat_hbm.at[pl.ds(0, n)], buf_ref.at[pl.ds(base, n), :], sem.at[slot]).wait()

    gates = [gate_ref[:, k:k + 1] for k in range(TOP_K)]
    cols = []
    for s in range(PACK_SLABS):
        lo_acc = jnp.zeros((tm, LANES), F32)
        hi_acc = jnp.zeros((tm, LANES), F32)
        for k in range(TOP_K):
            lo, hi = _unpack_words(buf_ref[pl.ds(base + k * tm * PACK_SLABS + s, tm, stride=PACK_SLABS), :])
            lo_acc = lo_acc + gates[k] * lo
            hi_acc = hi_acc + gates[k] * hi
        cols.append(lo_acc)
        cols.append(hi_acc)
    ffn = jnp.concatenate(cols, axis=1)
    y = _layer_norm_rows(alpha * h_ref[...] + ffn, g_ref[...], be_ref[...])
    o_ref[...] = y
    obf_ref[...] = y.astype(BF16)


def _combine_ln(pos_flat, rows, gates, h, g, be, alpha):
    t, d = h.shape
    tm = min(COMBINE_TM, t)
    last = t // tm - 1
    row = lambda i: (i, 0)
    fixed = lambda i: (0, 0)
    return pl.pallas_call(
        functools.partial(_combine_ln_kernel, alpha),
        grid=(t // tm,),
        in_specs=[
            pl.BlockSpec((tm * TOP_K,), lambda i: (i,), memory_space=pltpu.SMEM),
            pl.BlockSpec((tm * TOP_K,), lambda i: (jnp.minimum(i + 1, last),), memory_space=pltpu.SMEM),
            pl.BlockSpec(memory_space=pl.ANY),
            pl.BlockSpec(memory_space=pl.ANY),
            pl.BlockSpec((tm, LANES), row),
            pl.BlockSpec((tm, d), row),
            pl.BlockSpec((1, d), fixed),
            pl.BlockSpec((1, d), fixed),
        ],
        out_specs=[pl.BlockSpec((tm, d), row), pl.BlockSpec((tm, d), row)],
        out_shape=[jax.ShapeDtypeStruct((t, d), F32), jax.ShapeDtypeStruct((t, d), BF16)],
        scratch_shapes=[pltpu.VMEM((2 * tm * TOP_K * PACK_SLABS, LANES), U32), pltpu.SemaphoreType.DMA((2,))],
        compiler_params=_params(1),
        name="combine_ln",
    )(pos_flat, pos_flat, rows, rows.reshape(-1, LANES), gates, h, g.reshape(1, d), be.reshape(1, d))


def _moe_layer(h, h_packed, router_w, router_b, w1, b1, w2, b2, layer, g, be, alpha):
    t, _ = h.shape
    n_e = router_w.shape[1]
    bm = MOE_BM
    n_blocks = -(-(t * TOP_K) // bm) + n_e
    idx_o, gate_o, rank_o, cnt_o = _router(h, router_w, router_b)
    idx = idx_o[:, :TOP_K]
    counts = cnt_o[0, :n_e].astype(I32)
    pcounts = (counts + bm - 1) // bm * bm
    pend = jnp.cumsum(pcounts)
    pstart = pend - pcounts
    hot = idx[:, :, None] == jnp.arange(n_e, dtype=I32)
    pos = (jnp.sum(jnp.where(hot, pstart, 0), axis=-1) + rank_o[:, :TOP_K].astype(I32)).reshape(-1)
    blk_first = (pstart // bm).astype(I32)
    blk_cnt = (pcounts // bm).astype(I32)
    xs = _dispatch(h_packed.reshape(t, PACK_SLABS, LANES), pos, n_blocks * bm)
    act = _moe_up(blk_first, blk_cnt, xs.reshape(n_blocks * bm * PACK_SLABS, LANES), w1, b1, layer, n_blocks)
    rows = _moe_down(blk_first, blk_cnt, act, w2, b2, layer, n_blocks)
    return _combine_ln(pos, rows.reshape(n_blocks * bm, PACK_SLABS, LANES), gate_o, h, g, be, alpha)


def kernel(x, mem, positions, attn_w_in, attn_b_in, attn_sinks, ssd_w_in, ssd_b_in, ssd_conv_w, ssd_conv_b, ssd_dt_bias, ssd_a_log, ssd_d_skip, ssd_norm_g, conf_w_in, conf_b_in, conf_dw_w, conf_dw_b, conf_ln_g, conf_ln_b, mem_w_kv, w_out, b_out, ln1_g, ln1_b, router_w, router_b, moe_w1, moe_b1, moe_w2, moe_b2, ln2_g, ln2_b):
    bsz, seq, d = x.shape
    depth = w_out.shape[0]
    t = bsz * seq
    alpha = (2 * depth) ** 0.25
    n_mem = mem.shape[1]

    posf = positions.astype(F32).reshape(t, 1)
    inv_freq = ROPE_THETA ** (-jnp.arange(ROPE_HALF, dtype=F32) / ROPE_HALF)
    lane = jnp.arange(LANES) % HEAD_DIM
    freq_row = jnp.where(lane < ROPE_DIM, inv_freq[lane % ROPE_HALF], 0.0).reshape(1, LANES)
    mem2 = mem.reshape(bsz * n_mem, d)

    h = x.reshape(t, d)
    h_in = h
    for i in range(depth):
        kind, j = i % N_MIXERS, i // N_MIXERS
        if kind == 0:
            w_in, b_in = attn_w_in[j], attn_b_in[j]
            u = _linear(h_in, w_in[:, :-XA_WIDTH], b_in[:-XA_WIDTH], BF16)
            mix = _swa_mixer(u, posf, attn_sinks[j], freq_row, bsz, seq)
        elif kind == 1:
            w_in, b_in = ssd_w_in[j], ssd_b_in[j]
            z = _linear(h_in, w_in[:, :MIX_WIDTH], b_in[:MIX_WIDTH], BF16)
            xbc = _linear(h_in, w_in[:, MIX_WIDTH:MIX_WIDTH + SSD_CONV_CH],
                          b_in[MIX_WIDTH:MIX_WIDTH + SSD_CONV_CH], F32)
            dt_lo = MIX_WIDTH + SSD_CONV_CH
            pad = LANES - SSD_HEADS
            dt_raw = _linear(h_in, jnp.pad(w_in[:, dt_lo:dt_lo + SSD_HEADS], ((0, 0), (0, pad))),
                             jnp.pad(b_in[dt_lo:dt_lo + SSD_HEADS], (0, pad)), F32)
            mix = _ssd_mixer(z, xbc, dt_raw, ssd_conv_w[j], ssd_conv_b[j], ssd_dt_bias[j], ssd_a_log[j],
                             ssd_d_skip[j], ssd_norm_g[j], bsz, seq)
        else:
            w_in, b_in = conf_w_in[j], conf_b_in[j]
            u = _linear(h_in, w_in[:, :-XA_WIDTH], b_in[:-XA_WIDTH], BF16)
            mix = _conformer_mixer(u, conf_dw_w[j], conf_dw_b[j], conf_ln_g[j], conf_ln_b[j], bsz, seq)
        qm = _linear(h_in, w_in[:, -XA_WIDTH:], b_in[-XA_WIDTH:], BF16)
        kv = _linear(mem2, mem_w_kv[i], jnp.zeros((2 * XA_WIDTH,), F32), BF16)
        xa = _memory_attention(qm, kv, bsz, seq)
        h1, _, h1_packed = _outproj_ln(mix, xa, w_out[i], b_out[i], h, ln1_g[i], ln1_b[i], alpha)
        h, h_in = _moe_layer(h1, h1_packed, router_w[i], router_b[i], moe_w1, moe_b1, moe_w2, moe_b2, i,
                             ln2_g[i], ln2_b[i], alpha)
    return h.reshape(bsz, seq, d)
```

```python
import functools
import math

import jax
import jax.numpy as jnp
from jax import lax
from jax.experimental import pallas as pl
from jax.experimental.pallas import tpu as pltpu

F32 = jnp.float32
BF16 = jnp.bfloat16
U32 = jnp.uint32
I32 = jnp.int32

LANES = 128
SUBLANES = 8
PACK_SLABS = 8
PACK_SPAN = 2 * LANES

D_MODEL = 2048
N_MIXERS = 3
MIX_WIDTH = 1536
XA_HEADS = 4
XA_HEAD_DIM = 128
XA_WIDTH = XA_HEADS * XA_HEAD_DIM
HEAD_DIM = 64
N_Q_HEADS = MIX_WIDTH // HEAD_DIM
N_KV_HEADS = 3
Q_PER_KV = N_Q_HEADS // N_KV_HEADS
WINDOW = 128
ROPE_DIM = HEAD_DIM // 4
ROPE_HALF = ROPE_DIM // 2
ROPE_THETA = 500000.0
SSD_HEAD_DIM = 64
SSD_HEADS = MIX_WIDTH // SSD_HEAD_DIM
SSD_GROUPS = 4
SSD_HPG = SSD_HEADS // SSD_GROUPS
SSD_GROUP_WIDTH = SSD_HPG * SSD_HEAD_DIM
SSD_STATE = 128
SSD_CONV = 4
SSD_CHUNK = 128
SSD_BC = SSD_GROUPS * SSD_STATE
SSD_CONV_CH = MIX_WIDTH + 2 * SSD_BC
CONF_WIDTH = 31
CONF_HALO = 32
SSD_HALO = 8
TOP_K = 4
D_FF = 1024
SWIGLU_LIMIT = 7.0
SWIGLU_ALPHA = 1.702
LN_EPS = 1e-5

MOE_BM = 512
MOE_TF = 512
ROUTER_TM = 512
COMBINE_TM = 256
DISPATCH_TM = 512
DISPATCH_UNROLL = 8
CONF_TR = 256
XA_TQ = 1024
OUT_TM = 512

HI_MASK = 0xFFFF0000

V7X_VMEM_BYTES = 64 * 1024 * 1024
VMEM_LIMIT_BYTES = V7X_VMEM_BYTES * 7 // 8


def _params(n_axes, **kw):
    return pltpu.CompilerParams(dimension_semantics=("arbitrary",) * n_axes,
                                vmem_limit_bytes=VMEM_LIMIT_BYTES, **kw)


def _sigmoid(x):
    return 1.0 / (1.0 + jnp.exp(-x))


def _silu(x):
    return x * _sigmoid(x)


def _pick_tile(n, cap):
    best = LANES
    for t in range(LANES, cap + 1, LANES):
        if n % t == 0:
            best = t
    return best


def _pack_words(a, b):
    a = a.astype(BF16).astype(F32)
    b = b.astype(BF16).astype(F32)
    return (pltpu.bitcast(a, U32) >> 16) | (pltpu.bitcast(b, U32) & jnp.uint32(HI_MASK))


def _unpack_words(w):
    lo = pltpu.bitcast(w << 16, F32)
    hi = pltpu.bitcast(w & jnp.uint32(HI_MASK), F32)
    return lo, hi


def _layer_norm_rows(y, g, b):
    mu = jnp.mean(y, axis=-1, keepdims=True)
    d = y - mu
    var = jnp.mean(d * d, axis=-1, keepdims=True)
    return d * lax.rsqrt(var + LN_EPS) * g + b


def _linear_kernel(x_ref, w_ref, b_ref, o_ref):
    x = x_ref[...].astype(BF16)
    acc = jnp.dot(x, w_ref[...], preferred_element_type=F32) + b_ref[...]
    o_ref[...] = acc.astype(o_ref.dtype)


def _linear(x, w, b, out_dtype, tm=1024, tn_cap=1024):
    m, k = x.shape
    n = w.shape[1]
    tn = _pick_tile(n, tn_cap)
    tm = min(tm, m)
    return pl.pallas_call(
        _linear_kernel,
        grid=(n // tn, m // tm),
        in_specs=[
            pl.BlockSpec((tm, k), lambda j, i: (i, 0)),
            pl.BlockSpec((k, tn), lambda j, i: (0, j)),
            pl.BlockSpec((1, tn), lambda j, i: (0, j)),
        ],
        out_specs=pl.BlockSpec((tm, tn), lambda j, i: (i, j)),
        out_shape=jax.ShapeDtypeStruct((m, n), out_dtype),
        compiler_params=_params(2),
        name="linear",
    )(x, w.astype(BF16), b.reshape(1, n).astype(F32))


def _outproj_ln_kernel(alpha, mix_ref, xa_ref, wm_ref, wx_ref, b_ref, h_ref, g_ref, be_ref,
                       o_ref, obf_ref, opk_ref):
    sub = jnp.dot(mix_ref[...], wm_ref[...], preferred_element_type=F32)
    sub = sub + jnp.dot(xa_ref[...], wx_ref[...], preferred_element_type=F32) + b_ref[...]
    y = _layer_norm_rows(alpha * h_ref[...] + sub, g_ref[...], be_ref[...])
    o_ref[...] = y
    obf_ref[...] = y.astype(BF16)
    tm = y.shape[0]
    for s in range(PACK_SLABS):
        lo = y[:, PACK_SPAN * s:PACK_SPAN * s + LANES]
        hi = y[:, PACK_SPAN * s + LANES:PACK_SPAN * (s + 1)]
        opk_ref[pl.ds(s, tm, stride=PACK_SLABS), :] = _pack_words(lo, hi)


def _outproj_ln(mix, xa, w_out, b_out, h, g, be, alpha):
    t, d = h.shape
    tm = min(OUT_TM, t)
    wm = w_out[:MIX_WIDTH].astype(BF16)
    wx = w_out[MIX_WIDTH:].astype(BF16)
    row = lambda i: (i, 0)
    fixed = lambda i: (0, 0)
    return pl.pallas_call(
        functools.partial(_outproj_ln_kernel, alpha),
        grid=(t // tm,),
        in_specs=[
            pl.BlockSpec((tm, MIX_WIDTH), row),
            pl.BlockSpec((tm, XA_WIDTH), row),
            pl.BlockSpec((MIX_WIDTH, d), fixed),
            pl.BlockSpec((XA_WIDTH, d), fixed),
            pl.BlockSpec((1, d), fixed),
            pl.BlockSpec((tm, d), row),
            pl.BlockSpec((1, d), fixed),
            pl.BlockSpec((1, d), fixed),
        ],
        out_specs=[
            pl.BlockSpec((tm, d), row),
            pl.BlockSpec((tm, d), row),
            pl.BlockSpec((tm * PACK_SLABS, LANES), row),
        ],
        out_shape=[
            jax.ShapeDtypeStruct((t, d), F32),
            jax.ShapeDtypeStruct((t, d), BF16),
            jax.ShapeDtypeStruct((t * PACK_SLABS, LANES), U32),
        ],
        compiler_params=_params(1),
        name="outproj_ln",
    )(mix, xa, wm, wx, b_out.reshape(1, d), h, g.reshape(1, d), be.reshape(1, d))


def _xa_kernel(q_ref, k_ref, v_ref, o_ref):
    scale = XA_HEAD_DIM ** -0.5
    outs = []
    for hd in range(XA_HEADS):
        sl = slice(hd * XA_HEAD_DIM, (hd + 1) * XA_HEAD_DIM)
        s = lax.dot_general(q_ref[:, sl], k_ref[:, sl], (((1,), (1,)), ((), ())),
                            preferred_element_type=F32) * scale
        m = jnp.max(s, axis=-1, keepdims=True)
        p = jnp.exp(s - m)
        den = jnp.sum(p, axis=-1, keepdims=True)
        o = jnp.dot(p.astype(BF16), v_ref[:, sl], preferred_element_type=F32)
        outs.append(o / den)
    o_ref[...] = jnp.concatenate(outs, axis=-1).astype(o_ref.dtype)


def _memory_attention(qm, kv, bsz, seq):
    n_mem = kv.shape[0] // bsz
    tq = min(XA_TQ, seq)
    nq = seq // tq
    return pl.pallas_call(
        _xa_kernel,
        grid=(bsz, nq),
        in_specs=[
            pl.BlockSpec((tq, XA_WIDTH), lambda b, i: (b * nq + i, 0)),
            pl.BlockSpec((n_mem, XA_WIDTH), lambda b, i: (b, 0)),
            pl.BlockSpec((n_mem, XA_WIDTH), lambda b, i: (b, 1)),
        ],
        out_specs=pl.BlockSpec((tq, XA_WIDTH), lambda b, i: (b * nq + i, 0)),
        out_shape=jax.ShapeDtypeStruct((bsz * seq, XA_WIDTH), BF16),
        compiler_params=_params(2),
        name="memory_attention",
    )(qm, kv, kv)


def _rope_tables(pos, freq_row):
    ang = pos * freq_row
    c = jnp.cos(ang)
    s = jnp.sin(ang)
    d = lax.broadcasted_iota(I32, ang.shape, 1) % HEAD_DIM
    cmul = jnp.where(d < ROPE_DIM, c, 1.0)
    s_up = jnp.where(d < ROPE_HALF, -s, 0.0)
    s_dn = jnp.where((d >= ROPE_HALF) & (d < ROPE_DIM), s, 0.0)
    return cmul, s_up, s_dn


def _rope_chunk(t, tabs):
    cmul, s_up, s_dn = tabs
    up = pltpu.roll(t, LANES - ROPE_HALF, axis=1)
    dn = pltpu.roll(t, ROPE_HALF, axis=1)
    return t * cmul + up * s_up + dn * s_dn


def _swa_kernel(sinks_ref, q_ref, kvc_ref, posc_ref, freq_ref, o_ref, kv_win):
    n = pl.program_id(1)
    w = WINDOW
    tab_c = _rope_tables(posc_ref[...], freq_ref[...])
    lane = lax.broadcasted_iota(I32, (w, LANES), 1)
    first_half = lane < HEAD_DIM

    slot = n % 2

    @pl.when(n == 0)
    def _():
        kv_win[1] = jnp.zeros(kv_win.shape[1:], BF16)

    c0 = _rope_chunk(kvc_ref[:, 0:LANES].astype(F32), tab_c)
    c1 = kvc_ref[:, LANES:2 * LANES].astype(F32)
    c1 = jnp.where(first_half, _rope_chunk(c1, tab_c), c1)
    c2 = kvc_ref[:, 2 * LANES:3 * LANES].astype(F32)
    kv_cur = jnp.concatenate([c0, c1, c2], axis=1).astype(BF16)
    kv_win[slot] = kv_cur
    kv = jnp.concatenate([kv_win[1 - slot], kv_cur], axis=0)

    scale = HEAD_DIM ** -0.5
    qs = []
    for j in range(MIX_WIDTH // LANES):
        qj = _rope_chunk(q_ref[:, j * LANES:(j + 1) * LANES].astype(F32), tab_c) * scale
        qs.append(qj.astype(BF16))

    rows = Q_PER_KV * w
    qi = lax.broadcasted_iota(I32, (rows, 2 * w), 0) % w
    sj = lax.broadcasted_iota(I32, (rows, 2 * w), 1)
    rel = qi + w - sj
    mask = (rel >= 0) & (rel < w) & ((n > 0) | (sj >= w))

    outs = []
    for g in range(N_KV_HEADS):
        k_g = kv[:, g * HEAD_DIM:(g + 1) * HEAD_DIM]
        v_g = kv[:, (N_KV_HEADS + g) * HEAD_DIM:(N_KV_HEADS + g + 1) * HEAD_DIM]
        q_parts = []
        sink_parts = []
        for r in range(Q_PER_KV):
            hd = g * Q_PER_KV + r
            chunk = qs[hd // 2]
            q_parts.append(chunk[:, (hd % 2) * HEAD_DIM:(hd % 2 + 1) * HEAD_DIM])
            sink_parts.append(jnp.full((w, 1), sinks_ref[hd], F32))
        q_g = jnp.concatenate(q_parts, axis=0)
        sink = jnp.concatenate(sink_parts, axis=0)
        s = lax.dot_general(q_g, k_g, (((1,), (1,)), ((), ())), preferred_element_type=F32)
        s = jnp.where(mask, s, -jnp.inf)
        m = jnp.maximum(jnp.max(s, axis=-1, keepdims=True), sink)
        p = jnp.exp(s - m)
        den = jnp.sum(p, axis=-1, keepdims=True) + jnp.exp(sink - m)
        o = jnp.dot(p.astype(BF16), v_g, preferred_element_type=F32) / den
        for r in range(Q_PER_KV):
            outs.append(o[r * w:(r + 1) * w, :])
    o_ref[...] = jnp.concatenate(outs, axis=1).astype(o_ref.dtype)


def _swa_mixer(u, posf, sinks, freq_row, bsz, seq):
    nb = seq // WINDOW
    kv_w = 2 * N_KV_HEADS * HEAD_DIM
    kv_col = MIX_WIDTH // kv_w
    cur = lambda b, n, s: (b * nb + n, 0)
    return pl.pallas_call(
        _swa_kernel,
        grid_spec=pltpu.PrefetchScalarGridSpec(
            num_scalar_prefetch=1,
            grid=(bsz, nb),
            in_specs=[
                pl.BlockSpec((WINDOW, MIX_WIDTH), cur),
                pl.BlockSpec((WINDOW, kv_w), lambda b, n, s: (b * nb + n, kv_col)),
                pl.BlockSpec((WINDOW, 1), cur),
                pl.BlockSpec((1, LANES), lambda b, n, s: (0, 0)),
            ],
            out_specs=pl.BlockSpec((WINDOW, MIX_WIDTH), cur),
            scratch_shapes=[pltpu.VMEM((2, WINDOW, kv_w), BF16)],
        ),
        out_shape=jax.ShapeDtypeStruct((bsz * seq, MIX_WIDTH), BF16),
        compiler_params=_params(2),
        name="swa",
    )(sinks.astype(F32), u, u, posf, freq_row)


def _split3(x):
    p0 = x.astype(BF16)
    r = x - p0.astype(F32)
    p1 = r.astype(BF16)
    p2 = (r - p1.astype(F32)).astype(BF16)
    return p0, p1, p2


def _expand_heads(arr, g):
    rows = arr.shape[0]
    parts = [jnp.broadcast_to(arr[:, g * SSD_HPG + r:g * SSD_HPG + r + 1], (rows, SSD_HEAD_DIM))
             for r in range(SSD_HPG)]
    return jnp.concatenate(parts, axis=1)


def _ssd_kernel(z_ref, xbc_ref, dt_ref, cw_ref, cb_ref, dtb_ref, alog_ref, dskip_ref, ng_ref,
                o_ref, ext_ref, state_ref):
    c = pl.program_id(1)
    q = SSD_CHUNK

    @pl.when(c == 0)
    def _():
        ext_ref[0:SSD_HALO, :] = jnp.zeros((SSD_HALO, SSD_CONV_CH), F32)
        state_ref[...] = jnp.zeros(state_ref.shape, F32)

    ext_ref[SSD_HALO:SSD_HALO + q, :] = xbc_ref[...].astype(F32)
    conv = cb_ref[...]
    for k in range(SSD_CONV):
        conv = conv + cw_ref[k:k + 1, :] * ext_ref[pl.ds(SSD_HALO - (SSD_CONV - 1) + k, q), :]
    ext_ref[0:SSD_HALO, :] = ext_ref[q:q + SSD_HALO, :]
    xbc = _silu(conv)

    lane = lax.broadcasted_iota(I32, (q, LANES), 1)
    x = dt_ref[...] + dtb_ref[...]
    dt = jnp.maximum(x, 0.0) + jnp.log1p(jnp.exp(-jnp.abs(x)))
    a = jnp.where(lane < SSD_HEADS, dt * (-jnp.exp(alog_ref[...])), 0.0)

    row = lax.broadcasted_iota(I32, (q, q), 0)
    col = lax.broadcasted_iota(I32, (q, q), 1)
    causal = row >= col
    tri = causal.astype(BF16)
    a_cs = jnp.zeros((q, LANES), F32)
    for piece in _split3(a):
        a_cs = a_cs + jnp.dot(tri, piece, preferred_element_type=F32)
    a_cs_t = a_cs.T
    a_last = jnp.broadcast_to(a_cs[q - 1:q, :], (q, LANES))

    ys = []
    for g in range(SSD_GROUPS):
        xs_g = xbc[:, g * SSD_GROUP_WIDTH:(g + 1) * SSD_GROUP_WIDTH]
        b_g = xbc[:, MIX_WIDTH + g * SSD_STATE:MIX_WIDTH + (g + 1) * SSD_STATE].astype(BF16)
        c_g = xbc[:, MIX_WIDTH + SSD_BC + g * SSD_STATE:MIX_WIDTH + SSD_BC + (g + 1) * SSD_STATE].astype(BF16)
        cb = lax.dot_general(c_g, b_g, (((1,), (1,)), ((), ())), preferred_element_type=F32)
        acs_e = _expand_heads(a_cs, g)
        last_e = _expand_heads(a_last, g)
        xdt = xs_g * _expand_heads(dt, g)
        xdt_bf = xdt.astype(BF16)

        diag = []
        for r in range(SSD_HPG):
            hd = g * SSD_HPG + r
            seg = a_cs[:, hd:hd + 1] - a_cs_t[hd:hd + 1, :]
            decay = jnp.exp(jnp.where(causal, seg, -jnp.inf))
            mat = (cb * decay).astype(BF16)
            diag.append(jnp.dot(mat, xdt_bf[:, r * SSD_HEAD_DIM:(r + 1) * SSD_HEAD_DIM],
                                preferred_element_type=F32))
        y_diag = jnp.concatenate(diag, axis=1)

        prev = state_ref[g]
        y_off = jnp.dot(c_g, prev.astype(BF16), preferred_element_type=F32) * jnp.exp(acs_e)
        wgt = (xdt * jnp.exp(last_e - acs_e)).astype(BF16)
        upd = lax.dot_general(b_g, wgt, (((0,), (0,)), ((), ())), preferred_element_type=F32)
        state_ref[g] = prev * jnp.exp(last_e[0:1, :]) + upd
        ys.append(y_diag + y_off + dskip_ref[:, g * SSD_GROUP_WIDTH:(g + 1) * SSD_GROUP_WIDTH] * xs_g)

    y = jnp.concatenate(ys, axis=1) * _silu(z_ref[...].astype(F32))
    ms = jnp.mean(y * y, axis=-1, keepdims=True)
    o_ref[...] = (y * lax.rsqrt(ms + LN_EPS) * ng_ref[...]).astype(o_ref.dtype)


def _ssd_mixer(z, xbc, dt_raw, conv_w, conv_b, dt_bias, a_log, d_skip, norm_g, bsz, seq):
    nc = seq // SSD_CHUNK
    pad = LANES - SSD_HEADS
    row = lambda b, c: (b * nc + c, 0)
    fixed = lambda b, c: (0, 0)
    return pl.pallas_call(
        _ssd_kernel,
        grid=(bsz, nc),
        in_specs=[
            pl.BlockSpec((SSD_CHUNK, MIX_WIDTH), row),
            pl.BlockSpec((SSD_CHUNK, SSD_CONV_CH), row),
            pl.BlockSpec((SSD_CHUNK, LANES), row),
            pl.BlockSpec((SSD_CONV, SSD_CONV_CH), fixed),
            pl.BlockSpec((1, SSD_CONV_CH), fixed),
            pl.BlockSpec((1, LANES), fixed),
            pl.BlockSpec((1, LANES), fixed),
            pl.BlockSpec((1, MIX_WIDTH), fixed),
            pl.BlockSpec((1, MIX_WIDTH), fixed),
        ],
        out_specs=pl.BlockSpec((SSD_CHUNK, MIX_WIDTH), row),
        out_shape=jax.ShapeDtypeStruct((bsz * seq, MIX_WIDTH), BF16),
        scratch_shapes=[
            pltpu.VMEM((SSD_HALO + SSD_CHUNK, SSD_CONV_CH), F32),
            pltpu.VMEM((SSD_GROUPS, SSD_STATE, SSD_GROUP_WIDTH), F32),
        ],
        compiler_params=_params(2),
        name="ssd",
    )(z, xbc, dt_raw, conv_w, conv_b.reshape(1, -1),
      jnp.pad(dt_bias, (0, pad)).reshape(1, LANES), jnp.pad(a_log, (0, pad)).reshape(1, LANES),
      jnp.repeat(d_skip, SSD_HEAD_DIM).reshape(1, MIX_WIDTH), norm_g.reshape(1, MIX_WIDTH))


def _conformer_kernel(a_ref, b_ref, w_ref, wb_ref, g_ref, be_ref, o_ref, ext_ref, acc_ref, shift_ref):
    i = pl.program_id(1)
    tr = a_ref.shape[0]

    @pl.when(i == 0)
    def _():
        ext_ref[0:CONF_HALO, :] = jnp.zeros((CONF_HALO, MIX_WIDTH), F32)

    ext_ref[CONF_HALO:CONF_HALO + tr, :] = a_ref[...].astype(F32) * _sigmoid(b_ref[...].astype(F32))
    first = CONF_HALO - (CONF_WIDTH - 1)
    span = shift_ref.shape[1]
    for j in range(MIX_WIDTH // LANES):
        cs = slice(j * LANES, (j + 1) * LANES)
        for a in range(1, SUBLANES):
            shift_ref[a - 1] = ext_ref[pl.ds(a, span), cs]
        acc = jnp.broadcast_to(wb_ref[:, cs], (tr, LANES))
        for k in range(CONF_WIDTH):
            a = (first + k) % SUBLANES
            base = first + k - a
            tap = ext_ref[pl.ds(base, tr), cs] if a == 0 else shift_ref[a - 1, pl.ds(base, tr), :]
            acc = acc + w_ref[k:k + 1, cs] * tap
        acc_ref[:, cs] = acc
    ext_ref[0:CONF_HALO, :] = ext_ref[tr:tr + CONF_HALO, :]
    y = _layer_norm_rows(acc_ref[...], g_ref[...], be_ref[...])
    o_ref[...] = _silu(y).astype(o_ref.dtype)


def _conformer_mixer(u, dw_w, dw_b, ln_g, ln_b, bsz, seq):
    tr = min(CONF_TR, seq)
    nr = seq // tr
    fixed = lambda b, i: (0, 0)
    return pl.pallas_call(
        _conformer_kernel,
        grid=(bsz, nr),
        in_specs=[
            pl.BlockSpec((tr, MIX_WIDTH), lambda b, i: (b * nr + i, 0)),
            pl.BlockSpec((tr, MIX_WIDTH), lambda b, i: (b * nr + i, 1)),
            pl.BlockSpec((CONF_WIDTH, MIX_WIDTH), fixed),
            pl.BlockSpec((1, MIX_WIDTH), fixed),
            pl.BlockSpec((1, MIX_WIDTH), fixed),
            pl.BlockSpec((1, MIX_WIDTH), fixed),
        ],
        out_specs=pl.BlockSpec((tr, MIX_WIDTH), lambda b, i: (b * nr + i, 0)),
        out_shape=jax.ShapeDtypeStruct((bsz * seq, MIX_WIDTH), BF16),
        scratch_shapes=[
            pltpu.VMEM((CONF_HALO + tr, MIX_WIDTH), F32),
            pltpu.VMEM((tr, MIX_WIDTH), F32),
            pltpu.VMEM((SUBLANES - 1, tr + CONF_HALO - SUBLANES, LANES), F32),
        ],
        compiler_params=_params(2),
        name="conformer",
    )(u, u, dw_w, dw_b.reshape(1, -1), ln_g.reshape(1, -1), ln_b.reshape(1, -1))


def _router_kernel(n_experts, h_ref, w_ref, wlo_ref, b_ref, idx_ref, gate_ref, rank_ref, cnt_ref, run_ref):
    i = pl.program_id(0)
    tm = h_ref.shape[0]

    @pl.when(i == 0)
    def _():
        run_ref[...] = jnp.zeros(run_ref.shape, F32)

    x = h_ref[...]
    x_hi = x.astype(BF16)
    x_lo = (x - x_hi.astype(F32)).astype(BF16)
    logits = (jnp.dot(x_hi, w_ref[...], preferred_element_type=F32)
              + jnp.dot(x_lo, w_ref[...], preferred_element_type=F32)
              + jnp.dot(x_hi, wlo_ref[...], preferred_element_type=F32)) + b_ref[...]
    lane = lax.broadcasted_iota(I32, (tm, LANES), 1).astype(F32)
    cur = jnp.where(lane < n_experts, logits, -jnp.inf)
    vals, idxs, hots = [], [], []
    for _ in range(TOP_K):
        m = jnp.max(cur, axis=-1, keepdims=True)
        ix = jnp.min(jnp.where(cur == m, lane, float(LANES)), axis=-1, keepdims=True)
        hot = lane == ix
        vals.append(m)
        idxs.append(ix)
        hots.append(hot)
        cur = jnp.where(hot, -jnp.inf, cur)
    exps = [jnp.exp(v - vals[0]) for v in vals]
    den = exps[0]
    for e in exps[1:]:
        den = den + e

    r = lax.broadcasted_iota(I32, (tm, tm), 0)
    c = lax.broadcasted_iota(I32, (tm, tm), 1)
    before = (r > c).astype(BF16)
    base = run_ref[0:1, :]
    ranks = []
    for hot in hots:
        hot_f = hot.astype(F32)
        earlier = jnp.dot(before, hot.astype(BF16), preferred_element_type=F32)
        ranks.append(jnp.sum(hot_f * (earlier + base), axis=-1, keepdims=True))
        base = base + jnp.sum(hot_f, axis=0, keepdims=True)
    run_ref[0:1, :] = base
    cnt_ref[...] = jnp.broadcast_to(base, cnt_ref.shape)

    idx_out = jnp.zeros((tm, LANES), F32)
    gate_out = jnp.zeros((tm, LANES), F32)
    rank_out = jnp.zeros((tm, LANES), F32)
    for k in range(TOP_K):
        sel = lane == k
        idx_out = jnp.where(sel, idxs[k], idx_out)
        gate_out = jnp.where(sel, exps[k] / den, gate_out)
        rank_out = jnp.where(sel, ranks[k], rank_out)
    idx_ref[...] = idx_out.astype(I32)
    gate_ref[...] = gate_out
    rank_ref[...] = rank_out


def _router(h, router_w, router_b):
    t, d = h.shape
    n_experts = router_w.shape[1]
    tm = min(ROUTER_TM, t)
    w = jnp.pad(router_w.astype(F32), ((0, 0), (0, LANES - n_experts)))
    w_hi = w.astype(BF16)
    w_lo = (w - w_hi.astype(F32)).astype(BF16)
    b = jnp.pad(router_b.astype(F32), (0, LANES - n_experts)).reshape(1, LANES)
    row = lambda i: (i, 0)
    fixed = lambda i: (0, 0)
    return pl.pallas_call(
        functools.partial(_router_kernel, n_experts),
        grid=(t // tm,),
        in_specs=[pl.BlockSpec((tm, d), row), pl.BlockSpec((d, LANES), fixed), pl.BlockSpec((d, LANES), fixed),
                  pl.BlockSpec((1, LANES), fixed)],
        out_specs=[pl.BlockSpec((tm, LANES), row), pl.BlockSpec((tm, LANES), row),
                   pl.BlockSpec((tm, LANES), row), pl.BlockSpec((8, LANES), fixed)],
        out_shape=[jax.ShapeDtypeStruct((t, LANES), I32), jax.ShapeDtypeStruct((t, LANES), F32),
                   jax.ShapeDtypeStruct((t, LANES), F32), jax.ShapeDtypeStruct((8, LANES), F32)],
        scratch_shapes=[pltpu.VMEM((8, LANES), F32)],
        compiler_params=_params(1),
        name="router",
    )(h, w_hi, w_lo, b)


def _dispatch_kernel(pos_ref, src_ref, dst_in, dst_hbm, sem):
    del dst_in
    tm = src_ref.shape[0]

    def issue(jo, carry):
        for ji in range(DISPATCH_UNROLL):
            j = jo * DISPATCH_UNROLL + ji
            for k in range(TOP_K):
                pltpu.make_async_copy(src_ref.at[j], dst_hbm.at[pos_ref[j * TOP_K + k]],
                                      sem).start(priority=k % 2)
        return carry

    lax.fori_loop(0, tm // DISPATCH_UNROLL, issue, 0)
    n = tm * TOP_K
    pltpu.make_async_copy(dst_hbm.at[pl.ds(0, n)], dst_hbm.at[pl.ds(0, n)], sem).wait()


def _dispatch(h_packed, pos_flat, n_rows):
    t = h_packed.shape[0]
    tm = min(DISPATCH_TM, t)
    zeros = jnp.zeros((n_rows, PACK_SLABS, LANES), U32)
    return pl.pallas_call(
        _dispatch_kernel,
        grid=(t // tm,),
        in_specs=[
            pl.BlockSpec((tm * TOP_K,), lambda i: (i,), memory_space=pltpu.SMEM),
            pl.BlockSpec((tm, PACK_SLABS, LANES), lambda i: (i, 0, 0)),
            pl.BlockSpec(memory_space=pl.ANY),
        ],
        out_specs=pl.BlockSpec(memory_space=pl.ANY),
        out_shape=jax.ShapeDtypeStruct((n_rows, PACK_SLABS, LANES), U32),
        scratch_shapes=[pltpu.SemaphoreType.DMA(())],
        input_output_aliases={2: 0},
        compiler_params=_params(1, has_side_effects=True),
        name="dispatch",
    )(pos_flat, h_packed, zeros)


def _expert_changed(be_ref, i):
    return (i == 0) | (be_ref[i] != be_ref[jnp.maximum(i - 1, 0)])


def _moe_up_kernel(be_ref, nu_ref, x_ref, wg_ref, wl_ref, bg_ref, bl_ref, o_ref, wg_bf, wl_bf, x_bf):
    i = pl.program_id(1)
    bm = o_ref.shape[0]

    @pl.when(_expert_changed(be_ref, i))
    def _():
        wg_bf[...] = wg_ref[0].astype(BF16)
        wl_bf[...] = wl_ref[0].astype(BF16)

    @pl.when(i < nu_ref[0])
    def _():
        for s in range(PACK_SLABS):
            lo, hi = _unpack_words(x_ref[pl.ds(s, bm, stride=PACK_SLABS), :])
            x_bf[:, PACK_SPAN * s:PACK_SPAN * s + LANES] = lo.astype(BF16)
            x_bf[:, PACK_SPAN * s + LANES:PACK_SPAN * (s + 1)] = hi.astype(BF16)
        x = x_bf[...]
        glu = jnp.dot(x, wg_bf[...], preferred_element_type=F32) + bg_ref[0]
        lin = jnp.dot(x, wl_bf[...], preferred_element_type=F32) + bl_ref[0]
        glu = jnp.minimum(glu, SWIGLU_LIMIT)
        lin = jnp.clip(lin, -SWIGLU_LIMIT, SWIGLU_LIMIT)
        o_ref[...] = (glu * _sigmoid(SWIGLU_ALPHA * glu) * (lin + 1.0)).astype(o_ref.dtype)

    @pl.when(i >= nu_ref[0])
    def _():
        o_ref[...] = jnp.zeros(o_ref.shape, o_ref.dtype)


def _moe_up(block_e, n_used, xs_flat, w1, b1, layer, n_blocks):
    n_layers, n_e, d, two_ff = w1.shape
    d_ff = two_ff // 2
    tf = MOE_TF
    nt = d_ff // tf
    bm = MOE_BM
    b1r = b1.reshape(n_layers, n_e, 1, two_ff)
    return pl.pallas_call(
        _moe_up_kernel,
        grid_spec=pltpu.PrefetchScalarGridSpec(
            num_scalar_prefetch=2,
            grid=(nt, n_blocks),
            in_specs=[
                pl.BlockSpec((bm * PACK_SLABS, LANES), lambda n, i, be, nu: (i, 0)),
                pl.BlockSpec((None, 1, d, tf), lambda n, i, be, nu: (layer, be[i], 0, n)),
                pl.BlockSpec((None, 1, d, tf), lambda n, i, be, nu: (layer, be[i], 0, n + nt)),
                pl.BlockSpec((None, 1, 1, tf), lambda n, i, be, nu: (layer, be[i], 0, n)),
                pl.BlockSpec((None, 1, 1, tf), lambda n, i, be, nu: (layer, be[i], 0, n + nt)),
            ],
            out_specs=pl.BlockSpec((bm, tf), lambda n, i, be, nu: (i, n)),
            scratch_shapes=[pltpu.VMEM((d, tf), BF16), pltpu.VMEM((d, tf), BF16), pltpu.VMEM((bm, d), BF16)],
        ),
        out_shape=jax.ShapeDtypeStruct((n_blocks * bm, d_ff), BF16),
        compiler_params=_params(2),
        name="moe_up",
    )(block_e, n_used, xs_flat, w1, w1, b1r, b1r)


def _moe_down_kernel(be_ref, nu_ref, a_ref, w_ref, b_ref, o_ref, w_bf):
    i = pl.program_id(0)
    bm = a_ref.shape[0]

    @pl.when(_expert_changed(be_ref, i))
    def _():
        w_bf[...] = w_ref[0].astype(BF16)

    @pl.when(i < nu_ref[0])
    def _():
        y = jnp.dot(a_ref[...], w_bf[...], preferred_element_type=F32) + b_ref[0]
        for s in range(PACK_SLABS):
            lo = y[:, PACK_SPAN * s:PACK_SPAN * s + LANES]
            hi = y[:, PACK_SPAN * s + LANES:PACK_SPAN * (s + 1)]
            o_ref[pl.ds(s, bm, stride=PACK_SLABS), :] = _pack_words(lo, hi)

    @pl.when(i >= nu_ref[0])
    def _():
        o_ref[...] = jnp.zeros(o_ref.shape, o_ref.dtype)


def _moe_down(block_e, n_used, act, w2, b2, layer, n_blocks):
    n_layers, n_e, d_ff, d = w2.shape
    bm = MOE_BM
    return pl.pallas_call(
        _moe_down_kernel,
        grid_spec=pltpu.PrefetchScalarGridSpec(
            num_scalar_prefetch=2,
            grid=(n_blocks,),
            in_specs=[
                pl.BlockSpec((bm, d_ff), lambda i, be, nu: (i, 0)),
                pl.BlockSpec((None, 1, d_ff, d), lambda i, be, nu: (layer, be[i], 0, 0)),
                pl.BlockSpec((None, 1, 1, d), lambda i, be, nu: (layer, be[i], 0, 0)),
            ],
            out_specs=pl.BlockSpec((bm * PACK_SLABS, LANES), lambda i, be, nu: (i, 0)),
            scratch_shapes=[pltpu.VMEM((d_ff, d), BF16)],
        ),
        out_shape=jax.ShapeDtypeStruct((n_blocks * bm * PACK_SLABS, LANES), U32),
        compiler_params=_params(1),
        name="moe_down",
    )(block_e, n_used, act, w2, b2.reshape(n_layers, n_e, 1, d))


def _combine_ln_kernel(alpha, pos_ref, pos_next_ref, rows_hbm, rows_flat_hbm, gate_ref, h_ref, g_ref,
                       be_ref, o_ref, obf_ref, buf_a, buf_b, sem):
    i = pl.program_id(0)
    last = pl.num_programs(0) - 1
    tm = h_ref.shape[0]
    n = tm * TOP_K * PACK_SLABS

    def issue(p_ref, buf, sem_k, lo, hi):
        for j in range(lo, hi):
            for k in range(TOP_K):
                pltpu.make_async_copy(rows_hbm.at[p_ref[j * TOP_K + k]],
                                      buf.at[pl.ds((k * tm + j) * PACK_SLABS, PACK_SLABS), :],
                                      sem_k).start(priority=k % 2)

    def wait(buf, sem_k):
        pltpu.make_async_copy(rows_flat_hbm.at[pl.ds(0, n)], buf, sem_k).wait()

    def step(cur, nxt, cur_sem, nxt_sem, is_even):
        if is_even:
            @pl.when(i == 0)
            def _():
                issue(pos_ref, cur, cur_sem, 0, tm)

        wait(cur, cur_sem)
        gates = [gate_ref[:, k:k + 1] for k in range(TOP_K)]
        cols = []
        per = tm // PACK_SLABS
        for s in range(PACK_SLABS):
            lo_acc = jnp.zeros((tm, LANES), F32)
            hi_acc = jnp.zeros((tm, LANES), F32)
            for k in range(TOP_K):
                lo, hi = _unpack_words(cur[pl.ds(k * tm * PACK_SLABS + s, tm, stride=PACK_SLABS), :])
                lo_acc = lo_acc + gates[k] * lo
                hi_acc = hi_acc + gates[k] * hi
            cols.append(lo_acc)
            cols.append(hi_acc)
            issue(pos_next_ref, nxt, nxt_sem, s * per, (s + 1) * per)
        ffn = jnp.concatenate(cols, axis=1)
        y = _layer_norm_rows(alpha * h_ref[...] + ffn, g_ref[...], be_ref[...])
        o_ref[...] = y
        obf_ref[...] = y.astype(BF16)

        @pl.when(i == last)
        def _():
            wait(nxt, nxt_sem)

    @pl.when(i % 2 == 0)
    def _():
        step(buf_a, buf_b, sem.at[0], sem.at[1], True)

    @pl.when(i % 2 == 1)
    def _():
        step(buf_b, buf_a, sem.at[1], sem.at[0], False)


def _combine_ln(pos_flat, rows, gates, h, g, be, alpha):
    t, d = h.shape
    tm = min(COMBINE_TM, t)
    last = t // tm - 1
    row = lambda i: (i, 0)
    fixed = lambda i: (0, 0)
    return pl.pallas_call(
        functools.partial(_combine_ln_kernel, alpha),
        grid=(t // tm,),
        in_specs=[
            pl.BlockSpec((tm * TOP_K,), lambda i: (i,), memory_space=pltpu.SMEM),
            pl.BlockSpec((tm * TOP_K,), lambda i: (jnp.minimum(i + 1, last),), memory_space=pltpu.SMEM),
            pl.BlockSpec(memory_space=pl.ANY),
            pl.BlockSpec(memory_space=pl.ANY),
            pl.BlockSpec((tm, LANES), row),
            pl.BlockSpec((tm, d), row),
            pl.BlockSpec((1, d), fixed),
            pl.BlockSpec((1, d), fixed),
        ],
        out_specs=[pl.BlockSpec((tm, d), row), pl.BlockSpec((tm, d), row)],
        out_shape=[jax.ShapeDtypeStruct((t, d), F32), jax.ShapeDtypeStruct((t, d), BF16)],
        scratch_shapes=[pltpu.VMEM((tm * TOP_K * PACK_SLABS, LANES), U32),
                        pltpu.VMEM((tm * TOP_K * PACK_SLABS, LANES), U32), pltpu.SemaphoreType.DMA((2,))],
        compiler_params=_params(1),
        name="combine_ln",
    )(pos_flat, pos_flat, rows, rows.reshape(-1, LANES), gates, h, g.reshape(1, d), be.reshape(1, d))


def _moe_layer(h, h_packed, router_w, router_b, w1, b1, w2, b2, layer, g, be, alpha):
    t, _ = h.shape
    n_e = router_w.shape[1]
    bm = MOE_BM
    n_blocks = -(-(t * TOP_K) // bm) + n_e
    idx_o, gate_o, rank_o, cnt_o = _router(h, router_w, router_b)
    idx = idx_o[:, :TOP_K]
    counts = cnt_o[0, :n_e].astype(I32)
    pcounts = (counts + bm - 1) // bm * bm
    pend = jnp.cumsum(pcounts)
    pstart = pend - pcounts
    hot = idx[:, :, None] == jnp.arange(n_e, dtype=I32)
    pos = (jnp.sum(jnp.where(hot, pstart, 0), axis=-1) + rank_o[:, :TOP_K].astype(I32)).reshape(-1)
    blk_row = jnp.arange(n_blocks, dtype=I32) * bm
    block_e = jnp.minimum(jnp.sum((pend[None, :] <= blk_row[:, None]).astype(I32), axis=1), n_e - 1)
    n_used = (pend[-1:] // bm).astype(I32)
    xs = _dispatch(h_packed.reshape(t, PACK_SLABS, LANES), pos, n_blocks * bm)
    act = _moe_up(block_e, n_used, xs.reshape(n_blocks * bm * PACK_SLABS, LANES), w1, b1, layer, n_blocks)
    rows = _moe_down(block_e, n_used, act, w2, b2, layer, n_blocks)
    return _combine_ln(pos, rows.reshape(n_blocks * bm, PACK_SLABS, LANES), gate_o, h, g, be, alpha)


def kernel(x, mem, positions, attn_w_in, attn_b_in, attn_sinks, ssd_w_in, ssd_b_in, ssd_conv_w, ssd_conv_b, ssd_dt_bias, ssd_a_log, ssd_d_skip, ssd_norm_g, conf_w_in, conf_b_in, conf_dw_w, conf_dw_b, conf_ln_g, conf_ln_b, mem_w_kv, w_out, b_out, ln1_g, ln1_b, router_w, router_b, moe_w1, moe_b1, moe_w2, moe_b2, ln2_g, ln2_b):
    bsz, seq, d = x.shape
    depth = w_out.shape[0]
    t = bsz * seq
    alpha = (2 * depth) ** 0.25
    n_mem = mem.shape[1]

    posf = positions.astype(F32).reshape(t, 1)
    inv_freq = ROPE_THETA ** (-jnp.arange(ROPE_HALF, dtype=F32) / ROPE_HALF)
    lane = jnp.arange(LANES) % HEAD_DIM
    freq_row = jnp.where(lane < ROPE_DIM, inv_freq[lane % ROPE_HALF], 0.0).reshape(1, LANES)
    mem2 = mem.reshape(bsz * n_mem, d)

    h = x.reshape(t, d)
    h_in = h
    for i in range(depth):
        kind, j = i % N_MIXERS, i // N_MIXERS
        if kind == 0:
            w_in, b_in = attn_w_in[j], attn_b_in[j]
            u = _linear(h_in, w_in[:, :-XA_WIDTH], b_in[:-XA_WIDTH], BF16)
            mix = _swa_mixer(u, posf, attn_sinks[j], freq_row, bsz, seq)
        elif kind == 1:
            w_in, b_in = ssd_w_in[j], ssd_b_in[j]
            z = _linear(h_in, w_in[:, :MIX_WIDTH], b_in[:MIX_WIDTH], BF16)
            xbc = _linear(h_in, w_in[:, MIX_WIDTH:MIX_WIDTH + SSD_CONV_CH],
                          b_in[MIX_WIDTH:MIX_WIDTH + SSD_CONV_CH], F32)
            dt_lo = MIX_WIDTH + SSD_CONV_CH
            pad = LANES - SSD_HEADS
            dt_raw = _linear(h_in, jnp.pad(w_in[:, dt_lo:dt_lo + SSD_HEADS], ((0, 0), (0, pad))),
                             jnp.pad(b_in[dt_lo:dt_lo + SSD_HEADS], (0, pad)), F32)
            mix = _ssd_mixer(z, xbc, dt_raw, ssd_conv_w[j], ssd_conv_b[j], ssd_dt_bias[j], ssd_a_log[j],
                             ssd_d_skip[j], ssd_norm_g[j], bsz, seq)
        else:
            w_in, b_in = conf_w_in[j], conf_b_in[j]
            u = _linear(h_in, w_in[:, :-XA_WIDTH], b_in[:-XA_WIDTH], BF16)
            mix = _conformer_mixer(u, conf_dw_w[j], conf_dw_b[j], conf_ln_g[j], conf_ln_b[j], bsz, seq)
        qm = _linear(h_in, w_in[:, -XA_WIDTH:], b_in[-XA_WIDTH:], BF16)
        kv = _linear(mem2, mem_w_kv[i], jnp.zeros((2 * XA_WIDTH,), F32), BF16)
        xa = _memory_attention(qm, kv, bsz, seq)
        h1, _, h1_packed = _outproj_ln(mix, xa, w_out[i], b_out[i], h, ln1_g[i], ln1_b[i], alpha)
        h, h_in = _moe_layer(h1, h1_packed, router_w[i], router_b[i], moe_w1, moe_b1, moe_w2, moe_b2, i,
                             ln2_g[i], ln2_b[i], alpha)
    return h.reshape(bsz, seq, d)
```

```python
import functools
import math

import jax
import jax.numpy as jnp
from jax import lax
from jax.experimental import pallas as pl
from jax.experimental.pallas import tpu as pltpu

F32 = jnp.float32
BF16 = jnp.bfloat16
U32 = jnp.uint32
I32 = jnp.int32

LANES = 128
SUBLANES = 8
PACK_SLABS = 8
PACK_SPAN = 2 * LANES

D_MODEL = 2048
N_MIXERS = 3
MIX_WIDTH = 1536
XA_HEADS = 4
XA_HEAD_DIM = 128
XA_WIDTH = XA_HEADS * XA_HEAD_DIM
HEAD_DIM = 64
N_Q_HEADS = MIX_WIDTH // HEAD_DIM
N_KV_HEADS = 3
Q_PER_KV = N_Q_HEADS // N_KV_HEADS
WINDOW = 128
ROPE_DIM = HEAD_DIM // 4
ROPE_HALF = ROPE_DIM // 2
ROPE_THETA = 500000.0
SSD_HEAD_DIM = 64
SSD_HEADS = MIX_WIDTH // SSD_HEAD_DIM
SSD_GROUPS = 4
SSD_HPG = SSD_HEADS // SSD_GROUPS
SSD_GROUP_WIDTH = SSD_HPG * SSD_HEAD_DIM
SSD_STATE = 128
SSD_CONV = 4
SSD_CHUNK = 128
SSD_BC = SSD_GROUPS * SSD_STATE
SSD_CONV_CH = MIX_WIDTH + 2 * SSD_BC
CONF_WIDTH = 31
CONF_HALO = 32
SSD_HALO = 8
TOP_K = 4
D_FF = 1024
SWIGLU_LIMIT = 7.0
SWIGLU_ALPHA = 1.702
LN_EPS = 1e-5

MOE_BM = 512
MOE_TF = 512
ROUTER_TM = 512
COMBINE_TM = 256
DISPATCH_TM = 512
DISPATCH_UNROLL = 8
CONF_TR = 256
XA_TQ = 1024
OUT_TM = 512

HI_MASK = 0xFFFF0000

V7X_VMEM_BYTES = 64 * 1024 * 1024
VMEM_LIMIT_BYTES = V7X_VMEM_BYTES * 7 // 8


def _params(n_axes, **kw):
    return pltpu.CompilerParams(dimension_semantics=("arbitrary",) * n_axes,
                                vmem_limit_bytes=VMEM_LIMIT_BYTES, **kw)


def _sigmoid(x):
    return 1.0 / (1.0 + jnp.exp(-x))


def _silu(x):
    return x * _sigmoid(x)


def _pick_tile(n, cap):
    best = LANES
    for t in range(LANES, cap + 1, LANES):
        if n % t == 0:
            best = t
    return best


def _pack_words(a, b):
    a = a.astype(BF16).astype(F32)
    b = b.astype(BF16).astype(F32)
    return (pltpu.bitcast(a, U32) >> 16) | (pltpu.bitcast(b, U32) & jnp.uint32(HI_MASK))


def _unpack_words(w):
    lo = pltpu.bitcast(w << 16, F32)
    hi = pltpu.bitcast(w & jnp.uint32(HI_MASK), F32)
    return lo, hi


def _layer_norm_rows(y, g, b):
    mu = jnp.mean(y, axis=-1, keepdims=True)
    d = y - mu
    var = jnp.mean(d * d, axis=-1, keepdims=True)
    return d * lax.rsqrt(var + LN_EPS) * g + b


def _linear_kernel(x_ref, w_ref, b_ref, o_ref):
    x = x_ref[...].astype(BF16)
    acc = jnp.dot(x, w_ref[...], preferred_element_type=F32) + b_ref[...]
    o_ref[...] = acc.astype(o_ref.dtype)


def _linear(x, w, b, out_dtype, tm=1024, tn_cap=1024):
    m, k = x.shape
    n = w.shape[1]
    tn = _pick_tile(n, tn_cap)
    tm = min(tm, m)
    return pl.pallas_call(
        _linear_kernel,
        grid=(n // tn, m // tm),
        in_specs=[
            pl.BlockSpec((tm, k), lambda j, i: (i, 0)),
            pl.BlockSpec((k, tn), lambda j, i: (0, j)),
            pl.BlockSpec((1, tn), lambda j, i: (0, j)),
        ],
        out_specs=pl.BlockSpec((tm, tn), lambda j, i: (i, j)),
        out_shape=jax.ShapeDtypeStruct((m, n), out_dtype),
        compiler_params=_params(2),
        name="linear",
    )(x, w.astype(BF16), b.reshape(1, n).astype(F32))


def _outproj_ln_kernel(alpha, mix_ref, xa_ref, wm_ref, wx_ref, b_ref, h_ref, g_ref, be_ref,
                       o_ref, obf_ref, opk_ref):
    sub = jnp.dot(mix_ref[...], wm_ref[...], preferred_element_type=F32)
    sub = sub + jnp.dot(xa_ref[...], wx_ref[...], preferred_element_type=F32) + b_ref[...]
    y = _layer_norm_rows(alpha * h_ref[...] + sub, g_ref[...], be_ref[...])
    o_ref[...] = y
    obf_ref[...] = y.astype(BF16)
    tm = y.shape[0]
    for s in range(PACK_SLABS):
        lo = y[:, PACK_SPAN * s:PACK_SPAN * s + LANES]
        hi = y[:, PACK_SPAN * s + LANES:PACK_SPAN * (s + 1)]
        opk_ref[pl.ds(s, tm, stride=PACK_SLABS), :] = _pack_words(lo, hi)


def _outproj_ln(mix, xa, w_out, b_out, h, g, be, alpha):
    t, d = h.shape
    tm = min(OUT_TM, t)
    wm = w_out[:MIX_WIDTH].astype(BF16)
    wx = w_out[MIX_WIDTH:].astype(BF16)
    row = lambda i: (i, 0)
    fixed = lambda i: (0, 0)
    return pl.pallas_call(
        functools.partial(_outproj_ln_kernel, alpha),
        grid=(t // tm,),
        in_specs=[
            pl.BlockSpec((tm, MIX_WIDTH), row),
            pl.BlockSpec((tm, XA_WIDTH), row),
            pl.BlockSpec((MIX_WIDTH, d), fixed),
            pl.BlockSpec((XA_WIDTH, d), fixed),
            pl.BlockSpec((1, d), fixed),
            pl.BlockSpec((tm, d), row),
            pl.BlockSpec((1, d), fixed),
            pl.BlockSpec((1, d), fixed),
        ],
        out_specs=[
            pl.BlockSpec((tm, d), row),
            pl.BlockSpec((tm, d), row),
            pl.BlockSpec((tm * PACK_SLABS, LANES), row),
        ],
        out_shape=[
            jax.ShapeDtypeStruct((t, d), F32),
            jax.ShapeDtypeStruct((t, d), BF16),
            jax.ShapeDtypeStruct((t * PACK_SLABS, LANES), U32),
        ],
        compiler_params=_params(1),
        name="outproj_ln",
    )(mix, xa, wm, wx, b_out.reshape(1, d), h, g.reshape(1, d), be.reshape(1, d))


def _xa_kernel(q_ref, k_ref, v_ref, o_ref):
    scale = XA_HEAD_DIM ** -0.5
    outs = []
    for hd in range(XA_HEADS):
        sl = slice(hd * XA_HEAD_DIM, (hd + 1) * XA_HEAD_DIM)
        s = lax.dot_general(q_ref[:, sl], k_ref[:, sl], (((1,), (1,)), ((), ())),
                            preferred_element_type=F32) * scale
        m = jnp.max(s, axis=-1, keepdims=True)
        p = jnp.exp(s - m)
        den = jnp.sum(p, axis=-1, keepdims=True)
        o = jnp.dot(p.astype(BF16), v_ref[:, sl], preferred_element_type=F32)
        outs.append(o / den)
    o_ref[...] = jnp.concatenate(outs, axis=-1).astype(o_ref.dtype)


def _memory_attention(qm, kv, bsz, seq):
    n_mem = kv.shape[0] // bsz
    tq = min(XA_TQ, seq)
    nq = seq // tq
    return pl.pallas_call(
        _xa_kernel,
        grid=(bsz, nq),
        in_specs=[
            pl.BlockSpec((tq, XA_WIDTH), lambda b, i: (b * nq + i, 0)),
            pl.BlockSpec((n_mem, XA_WIDTH), lambda b, i: (b, 0)),
            pl.BlockSpec((n_mem, XA_WIDTH), lambda b, i: (b, 1)),
        ],
        out_specs=pl.BlockSpec((tq, XA_WIDTH), lambda b, i: (b * nq + i, 0)),
        out_shape=jax.ShapeDtypeStruct((bsz * seq, XA_WIDTH), BF16),
        compiler_params=_params(2),
        name="memory_attention",
    )(qm, kv, kv)


def _rope_tables(pos, freq_row):
    ang = pos * freq_row
    c = jnp.cos(ang)
    s = jnp.sin(ang)
    d = lax.broadcasted_iota(I32, ang.shape, 1) % HEAD_DIM
    cmul = jnp.where(d < ROPE_DIM, c, 1.0)
    s_up = jnp.where(d < ROPE_HALF, -s, 0.0)
    s_dn = jnp.where((d >= ROPE_HALF) & (d < ROPE_DIM), s, 0.0)
    return cmul, s_up, s_dn


def _rope_chunk(t, tabs):
    cmul, s_up, s_dn = tabs
    up = pltpu.roll(t, LANES - ROPE_HALF, axis=1)
    dn = pltpu.roll(t, ROPE_HALF, axis=1)
    return t * cmul + up * s_up + dn * s_dn


def _swa_kernel(sinks_ref, q_ref, kvc_ref, posc_ref, freq_ref, o_ref, kv_win):
    n = pl.program_id(1)
    w = WINDOW
    tab_c = _rope_tables(posc_ref[...], freq_ref[...])
    lane = lax.broadcasted_iota(I32, (w, LANES), 1)
    first_half = lane < HEAD_DIM

    slot = n % 2

    @pl.when(n == 0)
    def _():
        kv_win[1] = jnp.zeros(kv_win.shape[1:], BF16)

    c0 = _rope_chunk(kvc_ref[:, 0:LANES].astype(F32), tab_c)
    c1 = kvc_ref[:, LANES:2 * LANES].astype(F32)
    c1 = jnp.where(first_half, _rope_chunk(c1, tab_c), c1)
    c2 = kvc_ref[:, 2 * LANES:3 * LANES].astype(F32)
    kv_cur = jnp.concatenate([c0, c1, c2], axis=1).astype(BF16)
    kv_win[slot] = kv_cur
    kv = jnp.concatenate([kv_win[1 - slot], kv_cur], axis=0)

    scale = HEAD_DIM ** -0.5
    qs = []
    for j in range(MIX_WIDTH // LANES):
        qj = _rope_chunk(q_ref[:, j * LANES:(j + 1) * LANES].astype(F32), tab_c) * scale
        qs.append(qj.astype(BF16))

    rows = Q_PER_KV * w
    qi = lax.broadcasted_iota(I32, (rows, 2 * w), 0) % w
    sj = lax.broadcasted_iota(I32, (rows, 2 * w), 1)
    rel = qi + w - sj
    mask = (rel >= 0) & (rel < w) & ((n > 0) | (sj >= w))

    outs = []
    for g in range(N_KV_HEADS):
        k_g = kv[:, g * HEAD_DIM:(g + 1) * HEAD_DIM]
        v_g = kv[:, (N_KV_HEADS + g) * HEAD_DIM:(N_KV_HEADS + g + 1) * HEAD_DIM]
        q_parts = []
        sink_parts = []
        for r in range(Q_PER_KV):
            hd = g * Q_PER_KV + r
            chunk = qs[hd // 2]
            q_parts.append(chunk[:, (hd % 2) * HEAD_DIM:(hd % 2 + 1) * HEAD_DIM])
            sink_parts.append(jnp.full((w, 1), sinks_ref[hd], F32))
        q_g = jnp.concatenate(q_parts, axis=0)
        sink = jnp.concatenate(sink_parts, axis=0)
        s = lax.dot_general(q_g, k_g, (((1,), (1,)), ((), ())), preferred_element_type=F32)
        s = jnp.where(mask, s, -jnp.inf)
        m = jnp.maximum(jnp.max(s, axis=-1, keepdims=True), sink)
        p = jnp.exp(s - m)
        den = jnp.sum(p, axis=-1, keepdims=True) + jnp.exp(sink - m)
        o = jnp.dot(p.astype(BF16), v_g, preferred_element_type=F32) / den
        for r in range(Q_PER_KV):
            outs.append(o[r * w:(r + 1) * w, :])
    o_ref[...] = jnp.concatenate(outs, axis=1).astype(o_ref.dtype)


def _swa_mixer(u, posf, sinks, freq_row, bsz, seq):
    nb = seq // WINDOW
    kv_w = 2 * N_KV_HEADS * HEAD_DIM
    kv_col = MIX_WIDTH // kv_w
    cur = lambda b, n, s: (b * nb + n, 0)
    return pl.pallas_call(
        _swa_kernel,
        grid_spec=pltpu.PrefetchScalarGridSpec(
            num_scalar_prefetch=1,
            grid=(bsz, nb),
            in_specs=[
                pl.BlockSpec((WINDOW, MIX_WIDTH), cur),
                pl.BlockSpec((WINDOW, kv_w), lambda b, n, s: (b * nb + n, kv_col)),
                pl.BlockSpec((WINDOW, 1), cur),
                pl.BlockSpec((1, LANES), lambda b, n, s: (0, 0)),
            ],
            out_specs=pl.BlockSpec((WINDOW, MIX_WIDTH), cur),
            scratch_shapes=[pltpu.VMEM((2, WINDOW, kv_w), BF16)],
        ),
        out_shape=jax.ShapeDtypeStruct((bsz * seq, MIX_WIDTH), BF16),
        compiler_params=_params(2),
        name="swa",
    )(sinks.astype(F32), u, u, posf, freq_row)


def _split3(x):
    p0 = x.astype(BF16)
    r = x - p0.astype(F32)
    p1 = r.astype(BF16)
    p2 = (r - p1.astype(F32)).astype(BF16)
    return p0, p1, p2


def _expand_heads(arr, g):
    rows = arr.shape[0]
    parts = [jnp.broadcast_to(arr[:, g * SSD_HPG + r:g * SSD_HPG + r + 1], (rows, SSD_HEAD_DIM))
             for r in range(SSD_HPG)]
    return jnp.concatenate(parts, axis=1)


def _ssd_kernel(z_ref, xbc_ref, dt_ref, cw_ref, cb_ref, dtb_ref, alog_ref, dskip_ref, ng_ref,
                o_ref, ext_ref, state_ref):
    c = pl.program_id(1)
    q = SSD_CHUNK

    @pl.when(c == 0)
    def _():
        ext_ref[0:SSD_HALO, :] = jnp.zeros((SSD_HALO, SSD_CONV_CH), F32)
        state_ref[...] = jnp.zeros(state_ref.shape, F32)

    ext_ref[SSD_HALO:SSD_HALO + q, :] = xbc_ref[...].astype(F32)
    conv = cb_ref[...]
    for k in range(SSD_CONV):
        conv = conv + cw_ref[k:k + 1, :] * ext_ref[pl.ds(SSD_HALO - (SSD_CONV - 1) + k, q), :]
    ext_ref[0:SSD_HALO, :] = ext_ref[q:q + SSD_HALO, :]
    xbc = _silu(conv)

    lane = lax.broadcasted_iota(I32, (q, LANES), 1)
    x = dt_ref[...] + dtb_ref[...]
    dt = jnp.maximum(x, 0.0) + jnp.log1p(jnp.exp(-jnp.abs(x)))
    a = jnp.where(lane < SSD_HEADS, dt * (-jnp.exp(alog_ref[...])), 0.0)

    row = lax.broadcasted_iota(I32, (q, q), 0)
    col = lax.broadcasted_iota(I32, (q, q), 1)
    causal = row >= col
    tri = causal.astype(BF16)
    a_cs = jnp.zeros((q, LANES), F32)
    for piece in _split3(a):
        a_cs = a_cs + jnp.dot(tri, piece, preferred_element_type=F32)
    a_cs_t = a_cs.T
    a_last = jnp.broadcast_to(a_cs[q - 1:q, :], (q, LANES))

    ys = []
    for g in range(SSD_GROUPS):
        xs_g = xbc[:, g * SSD_GROUP_WIDTH:(g + 1) * SSD_GROUP_WIDTH]
        b_g = xbc[:, MIX_WIDTH + g * SSD_STATE:MIX_WIDTH + (g + 1) * SSD_STATE].astype(BF16)
        c_g = xbc[:, MIX_WIDTH + SSD_BC + g * SSD_STATE:MIX_WIDTH + SSD_BC + (g + 1) * SSD_STATE].astype(BF16)
        cb = lax.dot_general(c_g, b_g, (((1,), (1,)), ((), ())), preferred_element_type=F32)
        acs_e = _expand_heads(a_cs, g)
        last_e = _expand_heads(a_last, g)
        xdt = xs_g * _expand_heads(dt, g)
        xdt_bf = xdt.astype(BF16)

        diag = []
        for r in range(SSD_HPG):
            hd = g * SSD_HPG + r
            seg = a_cs[:, hd:hd + 1] - a_cs_t[hd:hd + 1, :]
            decay = jnp.exp(jnp.where(causal, seg, -jnp.inf))
            mat = (cb * decay).astype(BF16)
            diag.append(jnp.dot(mat, xdt_bf[:, r * SSD_HEAD_DIM:(r + 1) * SSD_HEAD_DIM],
                                preferred_element_type=F32))
        y_diag = jnp.concatenate(diag, axis=1)

        prev = state_ref[g]
        y_off = jnp.dot(c_g, prev.astype(BF16), preferred_element_type=F32) * jnp.exp(acs_e)
        wgt = (xdt * jnp.exp(last_e - acs_e)).astype(BF16)
        upd = lax.dot_general(b_g, wgt, (((0,), (0,)), ((), ())), preferred_element_type=F32)
        state_ref[g] = prev * jnp.exp(last_e[0:1, :]) + upd
        ys.append(y_diag + y_off + dskip_ref[:, g * SSD_GROUP_WIDTH:(g + 1) * SSD_GROUP_WIDTH] * xs_g)

    y = jnp.concatenate(ys, axis=1) * _silu(z_ref[...].astype(F32))
    ms = jnp.mean(y * y, axis=-1, keepdims=True)
    o_ref[...] = (y * lax.rsqrt(ms + LN_EPS) * ng_ref[...]).astype(o_ref.dtype)


def _ssd_mixer(z, xbc, dt_raw, conv_w, conv_b, dt_bias, a_log, d_skip, norm_g, bsz, seq):
    nc = seq // SSD_CHUNK
    pad = LANES - SSD_HEADS
    row = lambda b, c: (b * nc + c, 0)
    fixed = lambda b, c: (0, 0)
    return pl.pallas_call(
        _ssd_kernel,
        grid=(bsz, nc),
        in_specs=[
            pl.BlockSpec((SSD_CHUNK, MIX_WIDTH), row),
            pl.BlockSpec((SSD_CHUNK, SSD_CONV_CH), row),
            pl.BlockSpec((SSD_CHUNK, LANES), row),
            pl.BlockSpec((SSD_CONV, SSD_CONV_CH), fixed),
            pl.BlockSpec((1, SSD_CONV_CH), fixed),
            pl.BlockSpec((1, LANES), fixed),
            pl.BlockSpec((1, LANES), fixed),
            pl.BlockSpec((1, MIX_WIDTH), fixed),
            pl.BlockSpec((1, MIX_WIDTH), fixed),
        ],
        out_specs=pl.BlockSpec((SSD_CHUNK, MIX_WIDTH), row),
        out_shape=jax.ShapeDtypeStruct((bsz * seq, MIX_WIDTH), BF16),
        scratch_shapes=[
            pltpu.VMEM((SSD_HALO + SSD_CHUNK, SSD_CONV_CH), F32),
            pltpu.VMEM((SSD_GROUPS, SSD_STATE, SSD_GROUP_WIDTH), F32),
        ],
        compiler_params=_params(2),
        name="ssd",
    )(z, xbc, dt_raw, conv_w, conv_b.reshape(1, -1),
      jnp.pad(dt_bias, (0, pad)).reshape(1, LANES), jnp.pad(a_log, (0, pad)).reshape(1, LANES),
      jnp.repeat(d_skip, SSD_HEAD_DIM).reshape(1, MIX_WIDTH), norm_g.reshape(1, MIX_WIDTH))


def _conformer_kernel(a_ref, b_ref, w_ref, wb_ref, g_ref, be_ref, o_ref, ext_ref, acc_ref, shift_ref):
    i = pl.program_id(1)
    tr = a_ref.shape[0]

    @pl.when(i == 0)
    def _():
        ext_ref[0:CONF_HALO, :] = jnp.zeros((CONF_HALO, MIX_WIDTH), F32)

    ext_ref[CONF_HALO:CONF_HALO + tr, :] = a_ref[...].astype(F32) * _sigmoid(b_ref[...].astype(F32))
    first = CONF_HALO - (CONF_WIDTH - 1)
    span = shift_ref.shape[1]
    for j in range(MIX_WIDTH // LANES):
        cs = slice(j * LANES, (j + 1) * LANES)
        for a in range(1, SUBLANES):
            shift_ref[a - 1] = ext_ref[pl.ds(a, span), cs]
        acc = jnp.broadcast_to(wb_ref[:, cs], (tr, LANES))
        for k in range(CONF_WIDTH):
            a = (first + k) % SUBLANES
            base = first + k - a
            tap = ext_ref[pl.ds(base, tr), cs] if a == 0 else shift_ref[a - 1, pl.ds(base, tr), :]
            acc = acc + w_ref[k:k + 1, cs] * tap
        acc_ref[:, cs] = acc
    ext_ref[0:CONF_HALO, :] = ext_ref[tr:tr + CONF_HALO, :]
    y = _layer_norm_rows(acc_ref[...], g_ref[...], be_ref[...])
    o_ref[...] = _silu(y).astype(o_ref.dtype)


def _conformer_mixer(u, dw_w, dw_b, ln_g, ln_b, bsz, seq):
    tr = min(CONF_TR, seq)
    nr = seq // tr
    fixed = lambda b, i: (0, 0)
    return pl.pallas_call(
        _conformer_kernel,
        grid=(bsz, nr),
        in_specs=[
            pl.BlockSpec((tr, MIX_WIDTH), lambda b, i: (b * nr + i, 0)),
            pl.BlockSpec((tr, MIX_WIDTH), lambda b, i: (b * nr + i, 1)),
            pl.BlockSpec((CONF_WIDTH, MIX_WIDTH), fixed),
            pl.BlockSpec((1, MIX_WIDTH), fixed),
            pl.BlockSpec((1, MIX_WIDTH), fixed),
            pl.BlockSpec((1, MIX_WIDTH), fixed),
        ],
        out_specs=pl.BlockSpec((tr, MIX_WIDTH), lambda b, i: (b * nr + i, 0)),
        out_shape=jax.ShapeDtypeStruct((bsz * seq, MIX_WIDTH), BF16),
        scratch_shapes=[
            pltpu.VMEM((CONF_HALO + tr, MIX_WIDTH), F32),
            pltpu.VMEM((tr, MIX_WIDTH), F32),
            pltpu.VMEM((SUBLANES - 1, tr + CONF_HALO - SUBLANES, LANES), F32),
        ],
        compiler_params=_params(2),
        name="conformer",
    )(u, u, dw_w, dw_b.reshape(1, -1), ln_g.reshape(1, -1), ln_b.reshape(1, -1))


def _router_kernel(n_experts, h_ref, w_ref, wlo_ref, b_ref, idx_ref, gate_ref, rank_ref, cnt_ref, run_ref):
    i = pl.program_id(0)
    tm = h_ref.shape[0]

    @pl.when(i == 0)
    def _():
        run_ref[...] = jnp.zeros(run_ref.shape, F32)

    x = h_ref[...]
    x_hi = x.astype(BF16)
    x_lo = (x - x_hi.astype(F32)).astype(BF16)
    logits = (jnp.dot(x_hi, w_ref[...], preferred_element_type=F32)
              + jnp.dot(x_lo, w_ref[...], preferred_element_type=F32)
              + jnp.dot(x_hi, wlo_ref[...], preferred_element_type=F32)) + b_ref[...]
    lane = lax.broadcasted_iota(I32, (tm, LANES), 1).astype(F32)
    cur = jnp.where(lane < n_experts, logits, -jnp.inf)
    vals, idxs, hots = [], [], []
    for _ in range(TOP_K):
        m = jnp.max(cur, axis=-1, keepdims=True)
        ix = jnp.min(jnp.where(cur == m, lane, float(LANES)), axis=-1, keepdims=True)
        hot = lane == ix
        vals.append(m)
        idxs.append(ix)
        hots.append(hot)
        cur = jnp.where(hot, -jnp.inf, cur)
    exps = [jnp.exp(v - vals[0]) for v in vals]
    den = exps[0]
    for e in exps[1:]:
        den = den + e

    r = lax.broadcasted_iota(I32, (tm, tm), 0)
    c = lax.broadcasted_iota(I32, (tm, tm), 1)
    before = (r > c).astype(BF16)
    base = run_ref[0:1, :]
    ranks = []
    for hot in hots:
        hot_f = hot.astype(F32)
        earlier = jnp.dot(before, hot.astype(BF16), preferred_element_type=F32)
        ranks.append(jnp.sum(hot_f * (earlier + base), axis=-1, keepdims=True))
        base = base + jnp.sum(hot_f, axis=0, keepdims=True)
    run_ref[0:1, :] = base
    cnt_ref[...] = jnp.broadcast_to(base, cnt_ref.shape)

    idx_out = jnp.zeros((tm, LANES), F32)
    gate_out = jnp.zeros((tm, LANES), F32)
    rank_out = jnp.zeros((tm, LANES), F32)
    for k in range(TOP_K):
        sel = lane == k
        idx_out = jnp.where(sel, idxs[k], idx_out)
        gate_out = jnp.where(sel, exps[k] / den, gate_out)
        rank_out = jnp.where(sel, ranks[k], rank_out)
    idx_ref[...] = idx_out.astype(I32)
    gate_ref[...] = gate_out
    rank_ref[...] = rank_out


def _router(h, router_w, router_b):
    t, d = h.shape
    n_experts = router_w.shape[1]
    tm = min(ROUTER_TM, t)
    w = jnp.pad(router_w.astype(F32), ((0, 0), (0, LANES - n_experts)))
    w_hi = w.astype(BF16)
    w_lo = (w - w_hi.astype(F32)).astype(BF16)
    b = jnp.pad(router_b.astype(F32), (0, LANES - n_experts)).reshape(1, LANES)
    row = lambda i: (i, 0)
    fixed = lambda i: (0, 0)
    return pl.pallas_call(
        functools.partial(_router_kernel, n_experts),
        grid=(t // tm,),
        in_specs=[pl.BlockSpec((tm, d), row), pl.BlockSpec((d, LANES), fixed), pl.BlockSpec((d, LANES), fixed),
                  pl.BlockSpec((1, LANES), fixed)],
        out_specs=[pl.BlockSpec((tm, LANES), row), pl.BlockSpec((tm, LANES), row),
                   pl.BlockSpec((tm, LANES), row), pl.BlockSpec((8, LANES), fixed)],
        out_shape=[jax.ShapeDtypeStruct((t, LANES), I32), jax.ShapeDtypeStruct((t, LANES), F32),
                   jax.ShapeDtypeStruct((t, LANES), F32), jax.ShapeDtypeStruct((8, LANES), F32)],
        scratch_shapes=[pltpu.VMEM((8, LANES), F32)],
        compiler_params=_params(1),
        name="router",
    )(h, w_hi, w_lo, b)


def _dispatch_kernel(tail_ref, pad_ref, used_ref, pos_ref, src_ref, dst_hbm, zero_buf, sem, zero_sem):
    i = pl.program_id(0)
    tm = src_ref.shape[0]
    zr = zero_buf.shape[0]
    n_blocks = dst_hbm.shape[0] // zr

    @pl.when(i == 0)
    def _():
        zero_buf[...] = jnp.zeros(zero_buf.shape, zero_buf.dtype)

        def zero_copy(row, size):
            return pltpu.make_async_copy(zero_buf.at[pl.ds(0, size)], dst_hbm.at[pl.ds(row, size)], zero_sem)

        def padding(act):
            def body(e, carry):
                row, pad = tail_ref[e], pad_ref[e]
                size = zr // 2
                while size >= 1:
                    piece = pad & size

                    @pl.when(piece != 0)
                    def _(row=row, size=size):
                        act(zero_copy(row, size))

                    row = row + piece
                    size //= 2
                return carry
            return body

        def blocks(act):
            def body(b, carry):
                act(zero_copy(b * zr, zr))
                return carry
            return body

        n_e = tail_ref.shape[0]
        lax.fori_loop(0, n_e, padding(lambda c: c.start()), 0)
        lax.fori_loop(used_ref[0], n_blocks, blocks(lambda c: c.start()), 0)
        lax.fori_loop(0, n_e, padding(lambda c: c.wait()), 0)
        lax.fori_loop(used_ref[0], n_blocks, blocks(lambda c: c.wait()), 0)

    def issue(jo, carry):
        for ji in range(DISPATCH_UNROLL):
            j = jo * DISPATCH_UNROLL + ji
            for k in range(TOP_K):
                pltpu.make_async_copy(src_ref.at[j], dst_hbm.at[pos_ref[j * TOP_K + k]],
                                      sem).start(priority=k % 2)
        return carry

    lax.fori_loop(0, tm // DISPATCH_UNROLL, issue, 0)
    n = tm * TOP_K
    pltpu.make_async_copy(dst_hbm.at[pl.ds(0, n)], dst_hbm.at[pl.ds(0, n)], sem).wait()


def _dispatch(h_packed, pos_flat, tail_row, pad_len, n_used, n_rows):
    t = h_packed.shape[0]
    tm = min(DISPATCH_TM, t)
    return pl.pallas_call(
        _dispatch_kernel,
        grid_spec=pltpu.PrefetchScalarGridSpec(
            num_scalar_prefetch=3,
            grid=(t // tm,),
            in_specs=[
                pl.BlockSpec((tm * TOP_K,), lambda i, tr, pd, nu: (i,), memory_space=pltpu.SMEM),
                pl.BlockSpec((tm, PACK_SLABS, LANES), lambda i, tr, pd, nu: (i, 0, 0)),
            ],
            out_specs=pl.BlockSpec(memory_space=pl.ANY),
            scratch_shapes=[pltpu.VMEM((MOE_BM, PACK_SLABS, LANES), U32),
                            pltpu.SemaphoreType.DMA(()), pltpu.SemaphoreType.DMA(())],
        ),
        out_shape=jax.ShapeDtypeStruct((n_rows, PACK_SLABS, LANES), U32),
        compiler_params=_params(1, has_side_effects=True),
        name="dispatch",
    )(tail_row, pad_len, n_used, pos_flat, h_packed)


def _expert_changed(be_ref, i):
    return (i == 0) | (be_ref[i] != be_ref[jnp.maximum(i - 1, 0)])


def _moe_up_kernel(be_ref, nu_ref, x_ref, wg_ref, wl_ref, bg_ref, bl_ref, o_ref, wg_bf, wl_bf, x_bf):
    i = pl.program_id(1)
    bm = o_ref.shape[0]

    @pl.when(_expert_changed(be_ref, i))
    def _():
        wg_bf[...] = wg_ref[0].astype(BF16)
        wl_bf[...] = wl_ref[0].astype(BF16)

    @pl.when(i < nu_ref[0])
    def _():
        for s in range(PACK_SLABS):
            lo, hi = _unpack_words(x_ref[pl.ds(s, bm, stride=PACK_SLABS), :])
            x_bf[:, PACK_SPAN * s:PACK_SPAN * s + LANES] = lo.astype(BF16)
            x_bf[:, PACK_SPAN * s + LANES:PACK_SPAN * (s + 1)] = hi.astype(BF16)
        x = x_bf[...]
        glu = jnp.dot(x, wg_bf[...], preferred_element_type=F32) + bg_ref[0]
        lin = jnp.dot(x, wl_bf[...], preferred_element_type=F32) + bl_ref[0]
        glu = jnp.minimum(glu, SWIGLU_LIMIT)
        lin = jnp.clip(lin, -SWIGLU_LIMIT, SWIGLU_LIMIT)
        o_ref[...] = (glu * _sigmoid(SWIGLU_ALPHA * glu) * (lin + 1.0)).astype(o_ref.dtype)

    @pl.when(i >= nu_ref[0])
    def _():
        o_ref[...] = jnp.zeros(o_ref.shape, o_ref.dtype)


def _moe_up(block_e, n_used, xs_flat, w1, b1, layer, n_blocks):
    n_layers, n_e, d, two_ff = w1.shape
    d_ff = two_ff // 2
    tf = MOE_TF
    nt = d_ff // tf
    bm = MOE_BM
    b1r = b1.reshape(n_layers, n_e, 1, two_ff)
    return pl.pallas_call(
        _moe_up_kernel,
        grid_spec=pltpu.PrefetchScalarGridSpec(
            num_scalar_prefetch=2,
            grid=(nt, n_blocks),
            in_specs=[
                pl.BlockSpec((bm * PACK_SLABS, LANES), lambda n, i, be, nu: (i, 0)),
                pl.BlockSpec((None, 1, d, tf), lambda n, i, be, nu: (layer, be[i], 0, n)),
                pl.BlockSpec((None, 1, d, tf), lambda n, i, be, nu: (layer, be[i], 0, n + nt)),
                pl.BlockSpec((None, 1, 1, tf), lambda n, i, be, nu: (layer, be[i], 0, n)),
                pl.BlockSpec((None, 1, 1, tf), lambda n, i, be, nu: (layer, be[i], 0, n + nt)),
            ],
            out_specs=pl.BlockSpec((bm, tf), lambda n, i, be, nu: (i, n)),
            scratch_shapes=[pltpu.VMEM((d, tf), BF16), pltpu.VMEM((d, tf), BF16), pltpu.VMEM((bm, d), BF16)],
        ),
        out_shape=jax.ShapeDtypeStruct((n_blocks * bm, d_ff), BF16),
        compiler_params=_params(2),
        name="moe_up",
    )(block_e, n_used, xs_flat, w1, w1, b1r, b1r)


def _moe_down_kernel(be_ref, nu_ref, a_ref, w_ref, b_ref, o_ref, w_bf):
    i = pl.program_id(0)
    bm = a_ref.shape[0]

    @pl.when(_expert_changed(be_ref, i))
    def _():
        w_bf[...] = w_ref[0].astype(BF16)

    @pl.when(i < nu_ref[0])
    def _():
        y = jnp.dot(a_ref[...], w_bf[...], preferred_element_type=F32) + b_ref[0]
        for s in range(PACK_SLABS):
            lo = y[:, PACK_SPAN * s:PACK_SPAN * s + LANES]
            hi = y[:, PACK_SPAN * s + LANES:PACK_SPAN * (s + 1)]
            o_ref[pl.ds(s, bm, stride=PACK_SLABS), :] = _pack_words(lo, hi)

    @pl.when(i >= nu_ref[0])
    def _():
        o_ref[...] = jnp.zeros(o_ref.shape, o_ref.dtype)


def _moe_down(block_e, n_used, act, w2, b2, layer, n_blocks):
    n_layers, n_e, d_ff, d = w2.shape
    bm = MOE_BM
    return pl.pallas_call(
        _moe_down_kernel,
        grid_spec=pltpu.PrefetchScalarGridSpec(
            num_scalar_prefetch=2,
            grid=(n_blocks,),
            in_specs=[
                pl.BlockSpec((bm, d_ff), lambda i, be, nu: (i, 0)),
                pl.BlockSpec((None, 1, d_ff, d), lambda i, be, nu: (layer, be[i], 0, 0)),
                pl.BlockSpec((None, 1, 1, d), lambda i, be, nu: (layer, be[i], 0, 0)),
            ],
            out_specs=pl.BlockSpec((bm * PACK_SLABS, LANES), lambda i, be, nu: (i, 0)),
            scratch_shapes=[pltpu.VMEM((d_ff, d), BF16)],
        ),
        out_shape=jax.ShapeDtypeStruct((n_blocks * bm * PACK_SLABS, LANES), U32),
        compiler_params=_params(1),
        name="moe_down",
    )(block_e, n_used, act, w2, b2.reshape(n_layers, n_e, 1, d))


def _combine_ln_kernel(alpha, pos_ref, pos_next_ref, rows_hbm, rows_flat_hbm, gate_ref, h_ref, g_ref,
                       be_ref, o_ref, obf_ref, buf_a, buf_b, sem):
    i = pl.program_id(0)
    last = pl.num_programs(0) - 1
    tm = h_ref.shape[0]
    n = tm * TOP_K * PACK_SLABS

    def issue(p_ref, buf, sem_k, lo, hi):
        for j in range(lo, hi):
            for k in range(TOP_K):
                pltpu.make_async_copy(rows_hbm.at[p_ref[j * TOP_K + k]],
                                      buf.at[pl.ds((k * tm + j) * PACK_SLABS, PACK_SLABS), :],
                                      sem_k).start(priority=k % 2)

    def wait(buf, sem_k):
        pltpu.make_async_copy(rows_flat_hbm.at[pl.ds(0, n)], buf, sem_k).wait()

    def step(cur, nxt, cur_sem, nxt_sem, is_even):
        if is_even:
            @pl.when(i == 0)
            def _():
                issue(pos_ref, cur, cur_sem, 0, tm)

        wait(cur, cur_sem)
        gates = [gate_ref[:, k:k + 1] for k in range(TOP_K)]
        cols = []
        per = tm // PACK_SLABS
        for s in range(PACK_SLABS):
            lo_acc = jnp.zeros((tm, LANES), F32)
            hi_acc = jnp.zeros((tm, LANES), F32)
            for k in range(TOP_K):
                lo, hi = _unpack_words(cur[pl.ds(k * tm * PACK_SLABS + s, tm, stride=PACK_SLABS), :])
                lo_acc = lo_acc + gates[k] * lo
                hi_acc = hi_acc + gates[k] * hi
            cols.append(lo_acc)
            cols.append(hi_acc)
            issue(pos_next_ref, nxt, nxt_sem, s * per, (s + 1) * per)
        ffn = jnp.concatenate(cols, axis=1)
        y = _layer_norm_rows(alpha * h_ref[...] + ffn, g_ref[...], be_ref[...])
        o_ref[...] = y
        obf_ref[...] = y.astype(BF16)

        @pl.when(i == last)
        def _():
            wait(nxt, nxt_sem)

    @pl.when(i % 2 == 0)
    def _():
        step(buf_a, buf_b, sem.at[0], sem.at[1], True)

    @pl.when(i % 2 == 1)
    def _():
        step(buf_b, buf_a, sem.at[1], sem.at[0], False)


def _combine_ln(pos_flat, rows, gates, h, g, be, alpha):
    t, d = h.shape
    tm = min(COMBINE_TM, t)
    last = t // tm - 1
    row = lambda i: (i, 0)
    fixed = lambda i: (0, 0)
    return pl.pallas_call(
        functools.partial(_combine_ln_kernel, alpha),
        grid=(t // tm,),
        in_specs=[
            pl.BlockSpec((tm * TOP_K,), lambda i: (i,), memory_space=pltpu.SMEM),
            pl.BlockSpec((tm * TOP_K,), lambda i: (jnp.minimum(i + 1, last),), memory_space=pltpu.SMEM),
            pl.BlockSpec(memory_space=pl.ANY),
            pl.BlockSpec(memory_space=pl.ANY),
            pl.BlockSpec((tm, LANES), row),
            pl.BlockSpec((tm, d), row),
            pl.BlockSpec((1, d), fixed),
            pl.BlockSpec((1, d), fixed),
        ],
        out_specs=[pl.BlockSpec((tm, d), row), pl.BlockSpec((tm, d), row)],
        out_shape=[jax.ShapeDtypeStruct((t, d), F32), jax.ShapeDtypeStruct((t, d), BF16)],
        scratch_shapes=[pltpu.VMEM((tm * TOP_K * PACK_SLABS, LANES), U32),
                        pltpu.VMEM((tm * TOP_K * PACK_SLABS, LANES), U32), pltpu.SemaphoreType.DMA((2,))],
        compiler_params=_params(1),
        name="combine_ln",
    )(pos_flat, pos_flat, rows, rows.reshape(-1, LANES), gates, h, g.reshape(1, d), be.reshape(1, d))


def _moe_layer(h, h_packed, router_w, router_b, w1, b1, w2, b2, layer, g, be, alpha):
    t, _ = h.shape
    n_e = router_w.shape[1]
    bm = MOE_BM
    n_blocks = -(-(t * TOP_K) // bm) + n_e
    idx_o, gate_o, rank_o, cnt_o = _router(h, router_w, router_b)
    idx = idx_o[:, :TOP_K]
    counts = cnt_o[0, :n_e].astype(I32)
    pcounts = (counts + bm - 1) // bm * bm
    pend = jnp.cumsum(pcounts)
    pstart = pend - pcounts
    hot = idx[:, :, None] == jnp.arange(n_e, dtype=I32)
    pos = (jnp.sum(jnp.where(hot, pstart, 0), axis=-1) + rank_o[:, :TOP_K].astype(I32)).reshape(-1)
    blk_row = jnp.arange(n_blocks, dtype=I32) * bm
    block_e = jnp.minimum(jnp.sum((pend[None, :] <= blk_row[:, None]).astype(I32), axis=1), n_e - 1)
    n_used = (pend[-1:] // bm).astype(I32)
    tail_row = (pstart + counts).astype(I32)
    pad_len = (pcounts - counts).astype(I32)
    xs = _dispatch(h_packed.reshape(t, PACK_SLABS, LANES), pos, tail_row, pad_len, n_used, n_blocks * bm)
    act = _moe_up(block_e, n_used, xs.reshape(n_blocks * bm * PACK_SLABS, LANES), w1, b1, layer, n_blocks)
    rows = _moe_down(block_e, n_used, act, w2, b2, layer, n_blocks)
    return _combine_ln(pos, rows.reshape(n_blocks * bm, PACK_SLABS, LANES), gate_o, h, g, be, alpha)


def kernel(x, mem, positions, attn_w_in, attn_b_in, attn_sinks, ssd_w_in, ssd_b_in, ssd_conv_w, ssd_conv_b, ssd_dt_bias, ssd_a_log, ssd_d_skip, ssd_norm_g, conf_w_in, conf_b_in, conf_dw_w, conf_dw_b, conf_ln_g, conf_ln_b, mem_w_kv, w_out, b_out, ln1_g, ln1_b, router_w, router_b, moe_w1, moe_b1, moe_w2, moe_b2, ln2_g, ln2_b):
    bsz, seq, d = x.shape
    depth = w_out.shape[0]
    t = bsz * seq
    alpha = (2 * depth) ** 0.25
    n_mem = mem.shape[1]

    posf = positions.astype(F32).reshape(t, 1)
    inv_freq = ROPE_THETA ** (-jnp.arange(ROPE_HALF, dtype=F32) / ROPE_HALF)
    lane = jnp.arange(LANES) % HEAD_DIM
    freq_row = jnp.where(lane < ROPE_DIM, inv_freq[lane % ROPE_HALF], 0.0).reshape(1, LANES)
    mem2 = mem.reshape(bsz * n_mem, d)

    h = x.reshape(t, d)
    h_in = h
    for i in range(depth):
        kind, j = i % N_MIXERS, i // N_MIXERS
        if kind == 0:
            w_in, b_in = attn_w_in[j], attn_b_in[j]
            u = _linear(h_in, w_in[:, :-XA_WIDTH], b_in[:-XA_WIDTH], BF16)
            mix = _swa_mixer(u, posf, attn_sinks[j], freq_row, bsz, seq)
        elif kind == 1:
            w_in, b_in = ssd_w_in[j], ssd_b_in[j]
            z = _linear(h_in, w_in[:, :MIX_WIDTH], b_in[:MIX_WIDTH], BF16)
            xbc = _linear(h_in, w_in[:, MIX_WIDTH:MIX_WIDTH + SSD_CONV_CH],
                          b_in[MIX_WIDTH:MIX_WIDTH + SSD_CONV_CH], F32)
            dt_lo = MIX_WIDTH + SSD_CONV_CH
            pad = LANES - SSD_HEADS
            dt_raw = _linear(h_in, jnp.pad(w_in[:, dt_lo:dt_lo + SSD_HEADS], ((0, 0), (0, pad))),
                             jnp.pad(b_in[dt_lo:dt_lo + SSD_HEADS], (0, pad)), F32)
            mix = _ssd_mixer(z, xbc, dt_raw, ssd_conv_w[j], ssd_conv_b[j], ssd_dt_bias[j], ssd_a_log[j],
                             ssd_d_skip[j], ssd_norm_g[j], bsz, seq)
        else:
            w_in, b_in = conf_w_in[j], conf_b_in[j]
            u = _linear(h_in, w_in[:, :-XA_WIDTH], b_in[:-XA_WIDTH], BF16)
            mix = _conformer_mixer(u, conf_dw_w[j], conf_dw_b[j], conf_ln_g[j], conf_ln_b[j], bsz, seq)
        qm = _linear(h_in, w_in[:, -XA_WIDTH:], b_in[-XA_WIDTH:], BF16)
        kv = _linear(mem2, mem_w_kv[i], jnp.zeros((2 * XA_WIDTH,), F32), BF16)
        xa = _memory_attention(qm, kv, bsz, seq)
        h1, _, h1_packed = _outproj_ln(mix, xa, w_out[i], b_out[i], h, ln1_g[i], ln1_b[i], alpha)
        h, h_in = _moe_layer(h1, h1_packed, router_w[i], router_b[i], moe_w1, moe_b1, moe_w2, moe_b2, i,
                             ln2_g[i], ln2_b[i], alpha)
    return h.reshape(bsz, seq, d)
```

```python
import functools
import math

import jax
import jax.numpy as jnp
from jax import lax
from jax.experimental import pallas as pl
from jax.experimental.pallas import tpu as pltpu

F32 = jnp.float32
BF16 = jnp.bfloat16
U32 = jnp.uint32
I32 = jnp.int32

LANES = 128
SUBLANES = 8
PACK_SLABS = 8
PACK_SPAN = 2 * LANES

D_MODEL = 2048
N_MIXERS = 3
MIX_WIDTH = 1536
XA_HEADS = 4
XA_HEAD_DIM = 128
XA_WIDTH = XA_HEADS * XA_HEAD_DIM
HEAD_DIM = 64
N_Q_HEADS = MIX_WIDTH // HEAD_DIM
N_KV_HEADS = 3
Q_PER_KV = N_Q_HEADS // N_KV_HEADS
WINDOW = 128
ROPE_DIM = HEAD_DIM // 4
ROPE_HALF = ROPE_DIM // 2
ROPE_THETA = 500000.0
SSD_HEAD_DIM = 64
SSD_HEADS = MIX_WIDTH // SSD_HEAD_DIM
SSD_GROUPS = 4
SSD_HPG = SSD_HEADS // SSD_GROUPS
SSD_GROUP_WIDTH = SSD_HPG * SSD_HEAD_DIM
SSD_STATE = 128
SSD_CONV = 4
SSD_CHUNK = 128
SSD_BC = SSD_GROUPS * SSD_STATE
SSD_CONV_CH = MIX_WIDTH + 2 * SSD_BC
CONF_WIDTH = 31
CONF_HALO = 32
SSD_HALO = 8
TOP_K = 4
D_FF = 1024
SWIGLU_LIMIT = 7.0
SWIGLU_ALPHA = 1.702
LN_EPS = 1e-5

MOE_BM = 512
MOE_TF = 512
ROUTER_TM = 512
COMBINE_TM = 256
DISPATCH_TM = 512
DISPATCH_UNROLL = 8
CONF_TR = 256
SWA_TQ = 2 * WINDOW
XA_TQ = 1024
OUT_TM = 512

HI_MASK = 0xFFFF0000

V7X_VMEM_BYTES = 64 * 1024 * 1024
VMEM_LIMIT_BYTES = V7X_VMEM_BYTES * 7 // 8


def _params(n_axes, **kw):
    return pltpu.CompilerParams(dimension_semantics=("arbitrary",) * n_axes,
                                vmem_limit_bytes=VMEM_LIMIT_BYTES, **kw)


def _sigmoid(x):
    return 1.0 / (1.0 + jnp.exp(-x))


def _silu(x):
    return x * _sigmoid(x)


def _pick_tile(n, cap):
    best = LANES
    for t in range(LANES, cap + 1, LANES):
        if n % t == 0:
            best = t
    return best


def _pack_words(a, b):
    a = a.astype(BF16).astype(F32)
    b = b.astype(BF16).astype(F32)
    return (pltpu.bitcast(a, U32) >> 16) | (pltpu.bitcast(b, U32) & jnp.uint32(HI_MASK))


def _unpack_words(w):
    lo = pltpu.bitcast(w << 16, F32)
    hi = pltpu.bitcast(w & jnp.uint32(HI_MASK), F32)
    return lo, hi


def _layer_norm_rows(y, g, b):
    mu = jnp.mean(y, axis=-1, keepdims=True)
    d = y - mu
    var = jnp.mean(d * d, axis=-1, keepdims=True)
    return d * lax.rsqrt(var + LN_EPS) * g + b


def _linear_kernel(x_ref, w_ref, b_ref, o_ref):
    x = x_ref[...].astype(BF16)
    acc = jnp.dot(x, w_ref[...], preferred_element_type=F32) + b_ref[...]
    o_ref[...] = acc.astype(o_ref.dtype)


def _linear(x, w, b, out_dtype, tm=1024, tn_cap=1024):
    m, k = x.shape
    n = w.shape[1]
    tn = _pick_tile(n, tn_cap)
    tm = min(tm, m)
    return pl.pallas_call(
        _linear_kernel,
        grid=(n // tn, m // tm),
        in_specs=[
            pl.BlockSpec((tm, k), lambda j, i: (i, 0)),
            pl.BlockSpec((k, tn), lambda j, i: (0, j)),
            pl.BlockSpec((1, tn), lambda j, i: (0, j)),
        ],
        out_specs=pl.BlockSpec((tm, tn), lambda j, i: (i, j)),
        out_shape=jax.ShapeDtypeStruct((m, n), out_dtype),
        compiler_params=_params(2),
        name="linear",
    )(x, w.astype(BF16), b.reshape(1, n).astype(F32))


def _outproj_ln_kernel(alpha, mix_ref, xa_ref, wm_ref, wx_ref, b_ref, h_ref, g_ref, be_ref,
                       o_ref, obf_ref, opk_ref):
    sub = jnp.dot(mix_ref[...], wm_ref[...], preferred_element_type=F32)
    sub = sub + jnp.dot(xa_ref[...], wx_ref[...], preferred_element_type=F32) + b_ref[...]
    y = _layer_norm_rows(alpha * h_ref[...] + sub, g_ref[...], be_ref[...])
    o_ref[...] = y
    obf_ref[...] = y.astype(BF16)
    tm = y.shape[0]
    for s in range(PACK_SLABS):
        lo = y[:, PACK_SPAN * s:PACK_SPAN * s + LANES]
        hi = y[:, PACK_SPAN * s + LANES:PACK_SPAN * (s + 1)]
        opk_ref[pl.ds(s, tm, stride=PACK_SLABS), :] = _pack_words(lo, hi)


def _outproj_ln(mix, xa, w_out, b_out, h, g, be, alpha):
    t, d = h.shape
    tm = min(OUT_TM, t)
    wm = w_out[:MIX_WIDTH].astype(BF16)
    wx = w_out[MIX_WIDTH:].astype(BF16)
    row = lambda i: (i, 0)
    fixed = lambda i: (0, 0)
    return pl.pallas_call(
        functools.partial(_outproj_ln_kernel, alpha),
        grid=(t // tm,),
        in_specs=[
            pl.BlockSpec((tm, MIX_WIDTH), row),
            pl.BlockSpec((tm, XA_WIDTH), row),
            pl.BlockSpec((MIX_WIDTH, d), fixed),
            pl.BlockSpec((XA_WIDTH, d), fixed),
            pl.BlockSpec((1, d), fixed),
            pl.BlockSpec((tm, d), row),
            pl.BlockSpec((1, d), fixed),
            pl.BlockSpec((1, d), fixed),
        ],
        out_specs=[
            pl.BlockSpec((tm, d), row),
            pl.BlockSpec((tm, d), row),
            pl.BlockSpec((tm * PACK_SLABS, LANES), row),
        ],
        out_shape=[
            jax.ShapeDtypeStruct((t, d), F32),
            jax.ShapeDtypeStruct((t, d), BF16),
            jax.ShapeDtypeStruct((t * PACK_SLABS, LANES), U32),
        ],
        compiler_params=_params(1),
        name="outproj_ln",
    )(mix, xa, wm, wx, b_out.reshape(1, d), h, g.reshape(1, d), be.reshape(1, d))


def _xa_kernel(q_ref, k_ref, v_ref, o_ref):
    scale = XA_HEAD_DIM ** -0.5
    outs = []
    for hd in range(XA_HEADS):
        sl = slice(hd * XA_HEAD_DIM, (hd + 1) * XA_HEAD_DIM)
        s = lax.dot_general(q_ref[:, sl], k_ref[:, sl], (((1,), (1,)), ((), ())),
                            preferred_element_type=F32) * scale
        m = jnp.max(s, axis=-1, keepdims=True)
        p = jnp.exp(s - m)
        den = jnp.sum(p, axis=-1, keepdims=True)
        o = jnp.dot(p.astype(BF16), v_ref[:, sl], preferred_element_type=F32)
        outs.append(o / den)
    o_ref[...] = jnp.concatenate(outs, axis=-1).astype(o_ref.dtype)


def _memory_attention(qm, kv, bsz, seq):
    n_mem = kv.shape[0] // bsz
    tq = min(XA_TQ, seq)
    nq = seq // tq
    return pl.pallas_call(
        _xa_kernel,
        grid=(bsz, nq),
        in_specs=[
            pl.BlockSpec((tq, XA_WIDTH), lambda b, i: (b * nq + i, 0)),
            pl.BlockSpec((n_mem, XA_WIDTH), lambda b, i: (b, 0)),
            pl.BlockSpec((n_mem, XA_WIDTH), lambda b, i: (b, 1)),
        ],
        out_specs=pl.BlockSpec((tq, XA_WIDTH), lambda b, i: (b * nq + i, 0)),
        out_shape=jax.ShapeDtypeStruct((bsz * seq, XA_WIDTH), BF16),
        compiler_params=_params(2),
        name="memory_attention",
    )(qm, kv, kv)


def _rope_tables(pos, freq_row):
    ang = pos * freq_row
    c = jnp.cos(ang)
    s = jnp.sin(ang)
    d = lax.broadcasted_iota(I32, ang.shape, 1) % HEAD_DIM
    cmul = jnp.where(d < ROPE_DIM, c, 1.0)
    s_up = jnp.where(d < ROPE_HALF, -s, 0.0)
    s_dn = jnp.where((d >= ROPE_HALF) & (d < ROPE_DIM), s, 0.0)
    return cmul, s_up, s_dn


def _rope_chunk(t, tabs):
    cmul, s_up, s_dn = tabs
    up = pltpu.roll(t, LANES - ROPE_HALF, axis=1)
    dn = pltpu.roll(t, ROPE_HALF, axis=1)
    return t * cmul + up * s_up + dn * s_dn


def _swa_kernel(sinks_ref, q_ref, kvc_ref, posc_ref, freq_ref, o_ref, kv_win):
    n = pl.program_id(1)
    w = WINDOW
    tq = q_ref.shape[0]
    tab_c = _rope_tables(posc_ref[...], freq_ref[...])
    lane = lax.broadcasted_iota(I32, (tq, LANES), 1)
    first_half = lane < HEAD_DIM

    slot = n % 2

    @pl.when(n == 0)
    def _():
        kv_win[1] = jnp.zeros(kv_win.shape[1:], BF16)

    c0 = _rope_chunk(kvc_ref[:, 0:LANES].astype(F32), tab_c)
    c1 = kvc_ref[:, LANES:2 * LANES].astype(F32)
    c1 = jnp.where(first_half, _rope_chunk(c1, tab_c), c1)
    c2 = kvc_ref[:, 2 * LANES:3 * LANES].astype(F32)
    kv_cur = jnp.concatenate([c0, c1, c2], axis=1).astype(BF16)
    kv_win[slot] = kv_cur[tq - w:, :]
    kv_all = jnp.concatenate([kv_win[1 - slot], kv_cur], axis=0)

    scale = HEAD_DIM ** -0.5
    qs = []
    for j in range(MIX_WIDTH // LANES):
        qj = _rope_chunk(q_ref[:, j * LANES:(j + 1) * LANES].astype(F32), tab_c) * scale
        qs.append(qj.astype(BF16))

    rows = Q_PER_KV * w
    qi = lax.broadcasted_iota(I32, (rows, 2 * w), 0) % w
    sj = lax.broadcasted_iota(I32, (rows, 2 * w), 1)
    rel = qi + w - sj
    band = (rel >= 0) & (rel < w)
    band_first = band & ((n > 0) | (sj >= w))

    blocks = []
    for b in range(tq // w):
        kv = kv_all[b * w:(b + 2) * w, :]
        mask = band_first if b == 0 else band
        outs = []
        for g in range(N_KV_HEADS):
            k_g = kv[:, g * HEAD_DIM:(g + 1) * HEAD_DIM]
            v_g = kv[:, (N_KV_HEADS + g) * HEAD_DIM:(N_KV_HEADS + g + 1) * HEAD_DIM]
            q_parts = []
            sink_parts = []
            for r in range(Q_PER_KV):
                hd = g * Q_PER_KV + r
                chunk = qs[hd // 2]
                q_parts.append(chunk[b * w:(b + 1) * w, (hd % 2) * HEAD_DIM:(hd % 2 + 1) * HEAD_DIM])
                sink_parts.append(jnp.full((w, 1), sinks_ref[hd], F32))
            q_g = jnp.concatenate(q_parts, axis=0)
            sink = jnp.concatenate(sink_parts, axis=0)
            s = lax.dot_general(q_g, k_g, (((1,), (1,)), ((), ())), preferred_element_type=F32)
            s = jnp.where(mask, s, -jnp.inf)
            m = jnp.maximum(jnp.max(s, axis=-1, keepdims=True), sink)
            p = jnp.exp(s - m)
            den = jnp.sum(p, axis=-1, keepdims=True) + jnp.exp(sink - m)
            o = jnp.dot(p.astype(BF16), v_g, preferred_element_type=F32) / den
            for r in range(Q_PER_KV):
                outs.append(o[r * w:(r + 1) * w, :])
        blocks.append(jnp.concatenate(outs, axis=1))
    o_ref[...] = jnp.concatenate(blocks, axis=0).astype(o_ref.dtype)


def _swa_mixer(u, posf, sinks, freq_row, bsz, seq):
    tq = min(SWA_TQ, seq)
    nb = seq // tq
    kv_w = 2 * N_KV_HEADS * HEAD_DIM
    kv_col = MIX_WIDTH // kv_w
    cur = lambda b, n, s: (b * nb + n, 0)
    return pl.pallas_call(
        _swa_kernel,
        grid_spec=pltpu.PrefetchScalarGridSpec(
            num_scalar_prefetch=1,
            grid=(bsz, nb),
            in_specs=[
                pl.BlockSpec((tq, MIX_WIDTH), cur),
                pl.BlockSpec((tq, kv_w), lambda b, n, s: (b * nb + n, kv_col)),
                pl.BlockSpec((tq, 1), cur),
                pl.BlockSpec((1, LANES), lambda b, n, s: (0, 0)),
            ],
            out_specs=pl.BlockSpec((tq, MIX_WIDTH), cur),
            scratch_shapes=[pltpu.VMEM((2, WINDOW, kv_w), BF16)],
        ),
        out_shape=jax.ShapeDtypeStruct((bsz * seq, MIX_WIDTH), BF16),
        compiler_params=_params(2),
        name="swa",
    )(sinks.astype(F32), u, u, posf, freq_row)


def _split3(x):
    p0 = x.astype(BF16)
    r = x - p0.astype(F32)
    p1 = r.astype(BF16)
    p2 = (r - p1.astype(F32)).astype(BF16)
    return p0, p1, p2


def _expand_heads(arr, g):
    rows = arr.shape[0]
    parts = [jnp.broadcast_to(arr[:, g * SSD_HPG + r:g * SSD_HPG + r + 1], (rows, SSD_HEAD_DIM))
             for r in range(SSD_HPG)]
    return jnp.concatenate(parts, axis=1)


def _ssd_kernel(z_ref, xbc_ref, dt_ref, cw_ref, cb_ref, dtb_ref, alog_ref, dskip_ref, ng_ref,
                o_ref, ext_ref, state_ref):
    c = pl.program_id(1)
    q = SSD_CHUNK

    @pl.when(c == 0)
    def _():
        ext_ref[0:SSD_HALO, :] = jnp.zeros((SSD_HALO, SSD_CONV_CH), F32)
        state_ref[...] = jnp.zeros(state_ref.shape, F32)

    ext_ref[SSD_HALO:SSD_HALO + q, :] = xbc_ref[...].astype(F32)
    conv = cb_ref[...]
    for k in range(SSD_CONV):
        conv = conv + cw_ref[k:k + 1, :] * ext_ref[pl.ds(SSD_HALO - (SSD_CONV - 1) + k, q), :]
    ext_ref[0:SSD_HALO, :] = ext_ref[q:q + SSD_HALO, :]
    xbc = _silu(conv)

    lane = lax.broadcasted_iota(I32, (q, LANES), 1)
    x = dt_ref[...] + dtb_ref[...]
    dt = jnp.maximum(x, 0.0) + jnp.log1p(jnp.exp(-jnp.abs(x)))
    a = jnp.where(lane < SSD_HEADS, dt * (-jnp.exp(alog_ref[...])), 0.0)

    row = lax.broadcasted_iota(I32, (q, q), 0)
    col = lax.broadcasted_iota(I32, (q, q), 1)
    causal = row >= col
    tri = causal.astype(BF16)
    a_cs = jnp.zeros((q, LANES), F32)
    for piece in _split3(a):
        a_cs = a_cs + jnp.dot(tri, piece, preferred_element_type=F32)
    a_cs_t = a_cs.T
    a_last = jnp.broadcast_to(a_cs[q - 1:q, :], (q, LANES))

    ys = []
    for g in range(SSD_GROUPS):
        xs_g = xbc[:, g * SSD_GROUP_WIDTH:(g + 1) * SSD_GROUP_WIDTH]
        b_g = xbc[:, MIX_WIDTH + g * SSD_STATE:MIX_WIDTH + (g + 1) * SSD_STATE].astype(BF16)
        c_g = xbc[:, MIX_WIDTH + SSD_BC + g * SSD_STATE:MIX_WIDTH + SSD_BC + (g + 1) * SSD_STATE].astype(BF16)
        cb = lax.dot_general(c_g, b_g, (((1,), (1,)), ((), ())), preferred_element_type=F32)
        acs_e = _expand_heads(a_cs, g)
        last_e = _expand_heads(a_last, g)
        xdt = xs_g * _expand_heads(dt, g)
        xdt_bf = xdt.astype(BF16)

        diag = []
        for r in range(SSD_HPG):
            hd = g * SSD_HPG + r
            seg = a_cs[:, hd:hd + 1] - a_cs_t[hd:hd + 1, :]
            decay = jnp.exp(jnp.where(causal, seg, -jnp.inf))
            mat = (cb * decay).astype(BF16)
            diag.append(jnp.dot(mat, xdt_bf[:, r * SSD_HEAD_DIM:(r + 1) * SSD_HEAD_DIM],
                                preferred_element_type=F32))
        y_diag = jnp.concatenate(diag, axis=1)

        prev = state_ref[g]
        y_off = jnp.dot(c_g, prev.astype(BF16), preferred_element_type=F32) * jnp.exp(acs_e)
        wgt = (xdt * jnp.exp(last_e - acs_e)).astype(BF16)
        upd = lax.dot_general(b_g, wgt, (((0,), (0,)), ((), ())), preferred_element_type=F32)
        state_ref[g] = prev * jnp.exp(last_e[0:1, :]) + upd
        ys.append(y_diag + y_off + dskip_ref[:, g * SSD_GROUP_WIDTH:(g + 1) * SSD_GROUP_WIDTH] * xs_g)

    y = jnp.concatenate(ys, axis=1) * _silu(z_ref[...].astype(F32))
    ms = jnp.mean(y * y, axis=-1, keepdims=True)
    o_ref[...] = (y * lax.rsqrt(ms + LN_EPS) * ng_ref[...]).astype(o_ref.dtype)


def _ssd_mixer(z, xbc, dt_raw, conv_w, conv_b, dt_bias, a_log, d_skip, norm_g, bsz, seq):
    nc = seq // SSD_CHUNK
    pad = LANES - SSD_HEADS
    row = lambda b, c: (b * nc + c, 0)
    fixed = lambda b, c: (0, 0)
    return pl.pallas_call(
        _ssd_kernel,
        grid=(bsz, nc),
        in_specs=[
            pl.BlockSpec((SSD_CHUNK, MIX_WIDTH), row),
            pl.BlockSpec((SSD_CHUNK, SSD_CONV_CH), row),
            pl.BlockSpec((SSD_CHUNK, LANES), row),
            pl.BlockSpec((SSD_CONV, SSD_CONV_CH), fixed),
            pl.BlockSpec((1, SSD_CONV_CH), fixed),
            pl.BlockSpec((1, LANES), fixed),
            pl.BlockSpec((1, LANES), fixed),
            pl.BlockSpec((1, MIX_WIDTH), fixed),
            pl.BlockSpec((1, MIX_WIDTH), fixed),
        ],
        out_specs=pl.BlockSpec((SSD_CHUNK, MIX_WIDTH), row),
        out_shape=jax.ShapeDtypeStruct((bsz * seq, MIX_WIDTH), BF16),
        scratch_shapes=[
            pltpu.VMEM((SSD_HALO + SSD_CHUNK, SSD_CONV_CH), F32),
            pltpu.VMEM((SSD_GROUPS, SSD_STATE, SSD_GROUP_WIDTH), F32),
        ],
        compiler_params=_params(2),
        name="ssd",
    )(z, xbc, dt_raw, conv_w, conv_b.reshape(1, -1),
      jnp.pad(dt_bias, (0, pad)).reshape(1, LANES), jnp.pad(a_log, (0, pad)).reshape(1, LANES),
      jnp.repeat(d_skip, SSD_HEAD_DIM).reshape(1, MIX_WIDTH), norm_g.reshape(1, MIX_WIDTH))


def _conformer_kernel(a_ref, b_ref, w_ref, wb_ref, g_ref, be_ref, o_ref, ext_ref, acc_ref, shift_ref):
    i = pl.program_id(1)
    tr = a_ref.shape[0]

    @pl.when(i == 0)
    def _():
        ext_ref[0:CONF_HALO, :] = jnp.zeros((CONF_HALO, MIX_WIDTH), F32)

    ext_ref[CONF_HALO:CONF_HALO + tr, :] = a_ref[...].astype(F32) * _sigmoid(b_ref[...].astype(F32))
    first = CONF_HALO - (CONF_WIDTH - 1)
    span = shift_ref.shape[1]
    for j in range(MIX_WIDTH // LANES):
        cs = slice(j * LANES, (j + 1) * LANES)
        for a in range(1, SUBLANES):
            shift_ref[a - 1] = ext_ref[pl.ds(a, span), cs]
        acc = jnp.broadcast_to(wb_ref[:, cs], (tr, LANES))
        for k in range(CONF_WIDTH):
            a = (first + k) % SUBLANES
            base = first + k - a
            tap = ext_ref[pl.ds(base, tr), cs] if a == 0 else shift_ref[a - 1, pl.ds(base, tr), :]
            acc = acc + w_ref[k:k + 1, cs] * tap
        acc_ref[:, cs] = acc
    ext_ref[0:CONF_HALO, :] = ext_ref[tr:tr + CONF_HALO, :]
    y = _layer_norm_rows(acc_ref[...], g_ref[...], be_ref[...])
    o_ref[...] = _silu(y).astype(o_ref.dtype)


def _conformer_mixer(u, dw_w, dw_b, ln_g, ln_b, bsz, seq):
    tr = min(CONF_TR, seq)
    nr = seq // tr
    fixed = lambda b, i: (0, 0)
    return pl.pallas_call(
        _conformer_kernel,
        grid=(bsz, nr),
        in_specs=[
            pl.BlockSpec((tr, MIX_WIDTH), lambda b, i: (b * nr + i, 0)),
            pl.BlockSpec((tr, MIX_WIDTH), lambda b, i: (b * nr + i, 1)),
            pl.BlockSpec((CONF_WIDTH, MIX_WIDTH), fixed),
            pl.BlockSpec((1, MIX_WIDTH), fixed),
            pl.BlockSpec((1, MIX_WIDTH), fixed),
            pl.BlockSpec((1, MIX_WIDTH), fixed),
        ],
        out_specs=pl.BlockSpec((tr, MIX_WIDTH), lambda b, i: (b * nr + i, 0)),
        out_shape=jax.ShapeDtypeStruct((bsz * seq, MIX_WIDTH), BF16),
        scratch_shapes=[
            pltpu.VMEM((CONF_HALO + tr, MIX_WIDTH), F32),
            pltpu.VMEM((tr, MIX_WIDTH), F32),
            pltpu.VMEM((SUBLANES - 1, tr + CONF_HALO - SUBLANES, LANES), F32),
        ],
        compiler_params=_params(2),
        name="conformer",
    )(u, u, dw_w, dw_b.reshape(1, -1), ln_g.reshape(1, -1), ln_b.reshape(1, -1))


def _router_kernel(n_experts, h_ref, w_ref, wlo_ref, b_ref, idx_ref, gate_ref, rank_ref, cnt_ref, run_ref):
    i = pl.program_id(0)
    tm = h_ref.shape[0]

    @pl.when(i == 0)
    def _():
        run_ref[...] = jnp.zeros(run_ref.shape, F32)

    x = h_ref[...]
    x_hi = x.astype(BF16)
    x_lo = (x - x_hi.astype(F32)).astype(BF16)
    logits = (jnp.dot(x_hi, w_ref[...], preferred_element_type=F32)
              + jnp.dot(x_lo, w_ref[...], preferred_element_type=F32)
              + jnp.dot(x_hi, wlo_ref[...], preferred_element_type=F32)) + b_ref[...]
    lane = lax.broadcasted_iota(I32, (tm, LANES), 1).astype(F32)
    cur = jnp.where(lane < n_experts, logits, -jnp.inf)
    vals, idxs, hots = [], [], []
    for _ in range(TOP_K):
        m = jnp.max(cur, axis=-1, keepdims=True)
        ix = jnp.min(jnp.where(cur == m, lane, float(LANES)), axis=-1, keepdims=True)
        hot = lane == ix
        vals.append(m)
        idxs.append(ix)
        hots.append(hot)
        cur = jnp.where(hot, -jnp.inf, cur)
    exps = [jnp.exp(v - vals[0]) for v in vals]
    den = exps[0]
    for e in exps[1:]:
        den = den + e

    r = lax.broadcasted_iota(I32, (tm, tm), 0)
    c = lax.broadcasted_iota(I32, (tm, tm), 1)
    before = (r > c).astype(BF16)
    base = run_ref[0:1, :]
    ranks = []
    for hot in hots:
        hot_f = hot.astype(F32)
        earlier = jnp.dot(before, hot.astype(BF16), preferred_element_type=F32)
        ranks.append(jnp.sum(hot_f * (earlier + base), axis=-1, keepdims=True))
        base = base + jnp.sum(hot_f, axis=0, keepdims=True)
    run_ref[0:1, :] = base
    cnt_ref[...] = jnp.broadcast_to(base, cnt_ref.shape)

    idx_out = jnp.zeros((tm, LANES), F32)
    gate_out = jnp.zeros((tm, LANES), F32)
    rank_out = jnp.zeros((tm, LANES), F32)
    for k in range(TOP_K):
        sel = lane == k
        idx_out = jnp.where(sel, idxs[k], idx_out)
        gate_out = jnp.where(sel, exps[k] / den, gate_out)
        rank_out = jnp.where(sel, ranks[k], rank_out)
    idx_ref[...] = idx_out.astype(I32)
    gate_ref[...] = gate_out
    rank_ref[...] = rank_out


def _router(h, router_w, router_b):
    t, d = h.shape
    n_experts = router_w.shape[1]
    tm = min(ROUTER_TM, t)
    w = jnp.pad(router_w.astype(F32), ((0, 0), (0, LANES - n_experts)))
    w_hi = w.astype(BF16)
    w_lo = (w - w_hi.astype(F32)).astype(BF16)
    b = jnp.pad(router_b.astype(F32), (0, LANES - n_experts)).reshape(1, LANES)
    row = lambda i: (i, 0)
    fixed = lambda i: (0, 0)
    return pl.pallas_call(
        functools.partial(_router_kernel, n_experts),
        grid=(t // tm,),
        in_specs=[pl.BlockSpec((tm, d), row), pl.BlockSpec((d, LANES), fixed), pl.BlockSpec((d, LANES), fixed),
                  pl.BlockSpec((1, LANES), fixed)],
        out_specs=[pl.BlockSpec((tm, LANES), row), pl.BlockSpec((tm, LANES), row),
                   pl.BlockSpec((tm, LANES), row), pl.BlockSpec((8, LANES), fixed)],
        out_shape=[jax.ShapeDtypeStruct((t, LANES), I32), jax.ShapeDtypeStruct((t, LANES), F32),
                   jax.ShapeDtypeStruct((t, LANES), F32), jax.ShapeDtypeStruct((8, LANES), F32)],
        scratch_shapes=[pltpu.VMEM((8, LANES), F32)],
        compiler_params=_params(1),
        name="router",
    )(h, w_hi, w_lo, b)


def _dispatch_kernel(tail_ref, pad_ref, used_ref, pos_ref, src_ref, dst_hbm, zero_buf, sem, zero_sem):
    i = pl.program_id(0)
    tm = src_ref.shape[0]
    zr = zero_buf.shape[0]
    n_blocks = dst_hbm.shape[0] // zr

    @pl.when(i == 0)
    def _():
        zero_buf[...] = jnp.zeros(zero_buf.shape, zero_buf.dtype)

        def zero_copy(row, size):
            return pltpu.make_async_copy(zero_buf.at[pl.ds(0, size)], dst_hbm.at[pl.ds(row, size)], zero_sem)

        def padding(act):
            def body(e, carry):
                row, pad = tail_ref[e], pad_ref[e]
                size = zr // 2
                while size >= 1:
                    piece = pad & size

                    @pl.when(piece != 0)
                    def _(row=row, size=size):
                        act(zero_copy(row, size))

                    row = row + piece
                    size //= 2
                return carry
            return body

        def blocks(act):
            def body(b, carry):
                act(zero_copy(b * zr, zr))
                return carry
            return body

        n_e = tail_ref.shape[0]
        lax.fori_loop(0, n_e, padding(lambda c: c.start()), 0)
        lax.fori_loop(used_ref[0], n_blocks, blocks(lambda c: c.start()), 0)
        lax.fori_loop(0, n_e, padding(lambda c: c.wait()), 0)
        lax.fori_loop(used_ref[0], n_blocks, blocks(lambda c: c.wait()), 0)

    def issue(jo, carry):
        for ji in range(DISPATCH_UNROLL):
            j = jo * DISPATCH_UNROLL + ji
            for k in range(TOP_K):
                pltpu.make_async_copy(src_ref.at[j], dst_hbm.at[pos_ref[j * TOP_K + k]],
                                      sem).start(priority=k % 2)
        return carry

    lax.fori_loop(0, tm // DISPATCH_UNROLL, issue, 0)
    n = tm * TOP_K
    pltpu.make_async_copy(dst_hbm.at[pl.ds(0, n)], dst_hbm.at[pl.ds(0, n)], sem).wait()


def _dispatch(h_packed, pos_flat, tail_row, pad_len, n_used, n_rows):
    t = h_packed.shape[0]
    tm = min(DISPATCH_TM, t)
    return pl.pallas_call(
        _dispatch_kernel,
        grid_spec=pltpu.PrefetchScalarGridSpec(
            num_scalar_prefetch=3,
            grid=(t // tm,),
            in_specs=[
                pl.BlockSpec((tm * TOP_K,), lambda i, tr, pd, nu: (i,), memory_space=pltpu.SMEM),
                pl.BlockSpec((tm, PACK_SLABS, LANES), lambda i, tr, pd, nu: (i, 0, 0)),
            ],
            out_specs=pl.BlockSpec(memory_space=pl.ANY),
            scratch_shapes=[pltpu.VMEM((MOE_BM, PACK_SLABS, LANES), U32),
                            pltpu.SemaphoreType.DMA(()), pltpu.SemaphoreType.DMA(())],
        ),
        out_shape=jax.ShapeDtypeStruct((n_rows, PACK_SLABS, LANES), U32),
        compiler_params=_params(1, has_side_effects=True),
        name="dispatch",
    )(tail_row, pad_len, n_used, pos_flat, h_packed)


def _expert_changed(be_ref, i):
    return (i == 0) | (be_ref[i] != be_ref[jnp.maximum(i - 1, 0)])


def _moe_up_kernel(be_ref, nu_ref, x_ref, wg_ref, wl_ref, bg_ref, bl_ref, o_ref, wg_bf, wl_bf, x_bf):
    i = pl.program_id(1)
    bm = o_ref.shape[0]

    @pl.when(_expert_changed(be_ref, i))
    def _():
        wg_bf[...] = wg_ref[0].astype(BF16)
        wl_bf[...] = wl_ref[0].astype(BF16)

    @pl.when(i < nu_ref[0])
    def _():
        for s in range(PACK_SLABS):
            lo, hi = _unpack_words(x_ref[pl.ds(s, bm, stride=PACK_SLABS), :])
            x_bf[:, PACK_SPAN * s:PACK_SPAN * s + LANES] = lo.astype(BF16)
            x_bf[:, PACK_SPAN * s + LANES:PACK_SPAN * (s + 1)] = hi.astype(BF16)
        x = x_bf[...]
        glu = jnp.dot(x, wg_bf[...], preferred_element_type=F32) + bg_ref[0]
        lin = jnp.dot(x, wl_bf[...], preferred_element_type=F32) + bl_ref[0]
        glu = jnp.minimum(glu, SWIGLU_LIMIT)
        lin = jnp.clip(lin, -SWIGLU_LIMIT, SWIGLU_LIMIT)
        o_ref[...] = (glu * _sigmoid(SWIGLU_ALPHA * glu) * (lin + 1.0)).astype(o_ref.dtype)

    @pl.when(i >= nu_ref[0])
    def _():
        o_ref[...] = jnp.zeros(o_ref.shape, o_ref.dtype)


def _moe_up(block_e, n_used, xs_flat, w1, b1, layer, n_blocks):
    n_layers, n_e, d, two_ff = w1.shape
    d_ff = two_ff // 2
    tf = MOE_TF
    nt = d_ff // tf
    bm = MOE_BM
    b1r = b1.reshape(n_layers, n_e, 1, two_ff)
    return pl.pallas_call(
        _moe_up_kernel,
        grid_spec=pltpu.PrefetchScalarGridSpec(
            num_scalar_prefetch=2,
            grid=(nt, n_blocks),
            in_specs=[
                pl.BlockSpec((bm * PACK_SLABS, LANES), lambda n, i, be, nu: (i, 0)),
                pl.BlockSpec((None, 1, d, tf), lambda n, i, be, nu: (layer, be[i], 0, n)),
                pl.BlockSpec((None, 1, d, tf), lambda n, i, be, nu: (layer, be[i], 0, n + nt)),
                pl.BlockSpec((None, 1, 1, tf), lambda n, i, be, nu: (layer, be[i], 0, n)),
                pl.BlockSpec((None, 1, 1, tf), lambda n, i, be, nu: (layer, be[i], 0, n + nt)),
            ],
            out_specs=pl.BlockSpec((bm, tf), lambda n, i, be, nu: (i, n)),
            scratch_shapes=[pltpu.VMEM((d, tf), BF16), pltpu.VMEM((d, tf), BF16), pltpu.VMEM((bm, d), BF16)],
        ),
        out_shape=jax.ShapeDtypeStruct((n_blocks * bm, d_ff), BF16),
        compiler_params=_params(2),
        name="moe_up",
    )(block_e, n_used, xs_flat, w1, w1, b1r, b1r)


def _moe_down_kernel(be_ref, nu_ref, a_ref, w_ref, b_ref, o_ref, w_bf):
    i = pl.program_id(0)
    bm = a_ref.shape[0]

    @pl.when(_expert_changed(be_ref, i))
    def _():
        w_bf[...] = w_ref[0].astype(BF16)

    @pl.when(i < nu_ref[0])
    def _():
        y = jnp.dot(a_ref[...], w_bf[...], preferred_element_type=F32) + b_ref[0]
        for s in range(PACK_SLABS):
            lo = y[:, PACK_SPAN * s:PACK_SPAN * s + LANES]
            hi = y[:, PACK_SPAN * s + LANES:PACK_SPAN * (s + 1)]
            o_ref[pl.ds(s, bm, stride=PACK_SLABS), :] = _pack_words(lo, hi)

    @pl.when(i >= nu_ref[0])
    def _():
        o_ref[...] = jnp.zeros(o_ref.shape, o_ref.dtype)


def _moe_down(block_e, n_used, act, w2, b2, layer, n_blocks):
    n_layers, n_e, d_ff, d = w2.shape
    bm = MOE_BM
    return pl.pallas_call(
        _moe_down_kernel,
        grid_spec=pltpu.PrefetchScalarGridSpec(
            num_scalar_prefetch=2,
            grid=(n_blocks,),
            in_specs=[
                pl.BlockSpec((bm, d_ff), lambda i, be, nu: (i, 0)),
                pl.BlockSpec((None, 1, d_ff, d), lambda i, be, nu: (layer, be[i], 0, 0)),
                pl.BlockSpec((None, 1, 1, d), lambda i, be, nu: (layer, be[i], 0, 0)),
            ],
            out_specs=pl.BlockSpec((bm * PACK_SLABS, LANES), lambda i, be, nu: (i, 0)),
            scratch_shapes=[pltpu.VMEM((d_ff, d), BF16)],
        ),
        out_shape=jax.ShapeDtypeStruct((n_blocks * bm * PACK_SLABS, LANES), U32),
        compiler_params=_params(1),
        name="moe_down",
    )(block_e, n_used, act, w2, b2.reshape(n_layers, n_e, 1, d))


def _combine_ln_kernel(alpha, pos_ref, pos_next_ref, rows_hbm, rows_flat_hbm, gate_ref, h_ref, g_ref,
                       be_ref, o_ref, obf_ref, buf_a, buf_b, sem):
    i = pl.program_id(0)
    last = pl.num_programs(0) - 1
    tm = h_ref.shape[0]
    n = tm * TOP_K * PACK_SLABS

    def issue(p_ref, buf, sem_k, lo, hi):
        for j in range(lo, hi):
            for k in range(TOP_K):
                pltpu.make_async_copy(rows_hbm.at[p_ref[j * TOP_K + k]],
                                      buf.at[pl.ds((k * tm + j) * PACK_SLABS, PACK_SLABS), :],
                                      sem_k).start(priority=k % 2)

    def wait(buf, sem_k):
        pltpu.make_async_copy(rows_flat_hbm.at[pl.ds(0, n)], buf, sem_k).wait()

    def step(cur, nxt, cur_sem, nxt_sem, is_even):
        if is_even:
            @pl.when(i == 0)
            def _():
                issue(pos_ref, cur, cur_sem, 0, tm)

        wait(cur, cur_sem)
        gates = [gate_ref[:, k:k + 1] for k in range(TOP_K)]
        cols = []
        per = tm // PACK_SLABS
        for s in range(PACK_SLABS):
            lo_acc = jnp.zeros((tm, LANES), F32)
            hi_acc = jnp.zeros((tm, LANES), F32)
            for k in range(TOP_K):
                lo, hi = _unpack_words(cur[pl.ds(k * tm * PACK_SLABS + s, tm, stride=PACK_SLABS), :])
                lo_acc = lo_acc + gates[k] * lo
                hi_acc = hi_acc + gates[k] * hi
            cols.append(lo_acc)
            cols.append(hi_acc)
            issue(pos_next_ref, nxt, nxt_sem, s * per, (s + 1) * per)
        ffn = jnp.concatenate(cols, axis=1)
        y = _layer_norm_rows(alpha * h_ref[...] + ffn, g_ref[...], be_ref[...])
        o_ref[...] = y
        obf_ref[...] = y.astype(BF16)

        @pl.when(i == last)
        def _():
            wait(nxt, nxt_sem)

    @pl.when(i % 2 == 0)
    def _():
        step(buf_a, buf_b, sem.at[0], sem.at[1], True)

    @pl.when(i % 2 == 1)
    def _():
        step(buf_b, buf_a, sem.at[1], sem.at[0], False)


def _combine_ln(pos_flat, rows, gates, h, g, be, alpha):
    t, d = h.shape
    tm = min(COMBINE_TM, t)
    last = t // tm - 1
    row = lambda i: (i, 0)
    fixed = lambda i: (0, 0)
    return pl.pallas_call(
        functools.partial(_combine_ln_kernel, alpha),
        grid=(t // tm,),
        in_specs=[
            pl.BlockSpec((tm * TOP_K,), lambda i: (i,), memory_space=pltpu.SMEM),
            pl.BlockSpec((tm * TOP_K,), lambda i: (jnp.minimum(i + 1, last),), memory_space=pltpu.SMEM),
            pl.BlockSpec(memory_space=pl.ANY),
            pl.BlockSpec(memory_space=pl.ANY),
            pl.BlockSpec((tm, LANES), row),
            pl.BlockSpec((tm, d), row),
            pl.BlockSpec((1, d), fixed),
            pl.BlockSpec((1, d), fixed),
        ],
        out_specs=[pl.BlockSpec((tm, d), row), pl.BlockSpec((tm, d), row)],
        out_shape=[jax.ShapeDtypeStruct((t, d), F32), jax.ShapeDtypeStruct((t, d), BF16)],
        scratch_shapes=[pltpu.VMEM((tm * TOP_K * PACK_SLABS, LANES), U32),
                        pltpu.VMEM((tm * TOP_K * PACK_SLABS, LANES), U32), pltpu.SemaphoreType.DMA((2,))],
        compiler_params=_params(1),
        name="combine_ln",
    )(pos_flat, pos_flat, rows, rows.reshape(-1, LANES), gates, h, g.reshape(1, d), be.reshape(1, d))


def _moe_layer(h, h_packed, router_w, router_b, w1, b1, w2, b2, layer, g, be, alpha):
    t, _ = h.shape
    n_e = router_w.shape[1]
    bm = MOE_BM
    n_blocks = -(-(t * TOP_K) // bm) + n_e
    idx_o, gate_o, rank_o, cnt_o = _router(h, router_w, router_b)
    idx = idx_o[:, :TOP_K]
    counts = cnt_o[0, :n_e].astype(I32)
    pcounts = (counts + bm - 1) // bm * bm
    pend = jnp.cumsum(pcounts)
    pstart = pend - pcounts
    hot = idx[:, :, None] == jnp.arange(n_e, dtype=I32)
    pos = (jnp.sum(jnp.where(hot, pstart, 0), axis=-1) + rank_o[:, :TOP_K].astype(I32)).reshape(-1)
    blk_row = jnp.arange(n_blocks, dtype=I32) * bm
    block_e = jnp.minimum(jnp.sum((pend[None, :] <= blk_row[:, None]).astype(I32), axis=1), n_e - 1)
    n_used = (pend[-1:] // bm).astype(I32)
    tail_row = (pstart + counts).astype(I32)
    pad_len = (pcounts - counts).astype(I32)
    xs = _dispatch(h_packed.reshape(t, PACK_SLABS, LANES), pos, tail_row, pad_len, n_used, n_blocks * bm)
    act = _moe_up(block_e, n_used, xs.reshape(n_blocks * bm * PACK_SLABS, LANES), w1, b1, layer, n_blocks)
    rows = _moe_down(block_e, n_used, act, w2, b2, layer, n_blocks)
    return _combine_ln(pos, rows.reshape(n_blocks * bm, PACK_SLABS, LANES), gate_o, h, g, be, alpha)


def kernel(x, mem, positions, attn_w_in, attn_b_in, attn_sinks, ssd_w_in, ssd_b_in, ssd_conv_w, ssd_conv_b, ssd_dt_bias, ssd_a_log, ssd_d_skip, ssd_norm_g, conf_w_in, conf_b_in, conf_dw_w, conf_dw_b, conf_ln_g, conf_ln_b, mem_w_kv, w_out, b_out, ln1_g, ln1_b, router_w, router_b, moe_w1, moe_b1, moe_w2, moe_b2, ln2_g, ln2_b):
    bsz, seq, d = x.shape
    depth = w_out.shape[0]
    t = bsz * seq
    alpha = (2 * depth) ** 0.25
    n_mem = mem.shape[1]

    posf = positions.astype(F32).reshape(t, 1)
    inv_freq = ROPE_THETA ** (-jnp.arange(ROPE_HALF, dtype=F32) / ROPE_HALF)
    lane = jnp.arange(LANES) % HEAD_DIM
    freq_row = jnp.where(lane < ROPE_DIM, inv_freq[lane % ROPE_HALF], 0.0).reshape(1, LANES)
    mem2 = mem.reshape(bsz * n_mem, d)

    h = x.reshape(t, d)
    h_in = h
    for i in range(depth):
        kind, j = i % N_MIXERS, i // N_MIXERS
        if kind == 0:
            w_in, b_in = attn_w_in[j], attn_b_in[j]
            u = _linear(h_in, w_in[:, :-XA_WIDTH], b_in[:-XA_WIDTH], BF16)
            mix = _swa_mixer(u, posf, attn_sinks[j], freq_row, bsz, seq)
        elif kind == 1:
            w_in, b_in = ssd_w_in[j], ssd_b_in[j]
            z = _linear(h_in, w_in[:, :MIX_WIDTH], b_in[:MIX_WIDTH], BF16)
            xbc = _linear(h_in, w_in[:, MIX_WIDTH:MIX_WIDTH + SSD_CONV_CH],
                          b_in[MIX_WIDTH:MIX_WIDTH + SSD_CONV_CH], F32)
            dt_lo = MIX_WIDTH + SSD_CONV_CH
            pad = LANES - SSD_HEADS
            dt_raw = _linear(h_in, jnp.pad(w_in[:, dt_lo:dt_lo + SSD_HEADS], ((0, 0), (0, pad))),
                             jnp.pad(b_in[dt_lo:dt_lo + SSD_HEADS], (0, pad)), F32)
            mix = _ssd_mixer(z, xbc, dt_raw, ssd_conv_w[j], ssd_conv_b[j], ssd_dt_bias[j], ssd_a_log[j],
                             ssd_d_skip[j], ssd_norm_g[j], bsz, seq)
        else:
            w_in, b_in = conf_w_in[j], conf_b_in[j]
            u = _linear(h_in, w_in[:, :-XA_WIDTH], b_in[:-XA_WIDTH], BF16)
            mix = _conformer_mixer(u, conf_dw_w[j], conf_dw_b[j], conf_ln_g[j], conf_ln_b[j], bsz, seq)
        qm = _linear(h_in, w_in[:, -XA_WIDTH:], b_in[-XA_WIDTH:], BF16)
        kv = _linear(mem2, mem_w_kv[i], jnp.zeros((2 * XA_WIDTH,), F32), BF16)
        xa = _memory_attention(qm, kv, bsz, seq)
        h1, _, h1_packed = _outproj_ln(mix, xa, w_out[i], b_out[i], h, ln1_g[i], ln1_b[i], alpha)
        h, h_in = _moe_layer(h1, h1_packed, router_w[i], router_b[i], moe_w1, moe_b1, moe_w2, moe_b2, i,
                             ln2_g[i], ln2_b[i], alpha)
    return h.reshape(bsz, seq, d)
```

```python
import functools
import math

import jax
import jax.numpy as jnp
from jax import lax
from jax.experimental import pallas as pl
from jax.experimental.pallas import tpu as pltpu

F32 = jnp.float32
BF16 = jnp.bfloat16
U32 = jnp.uint32
I32 = jnp.int32

LANES = 128
SUBLANES = 8
PACK_SLABS = 8
PACK_SPAN = 2 * LANES

D_MODEL = 2048
N_MIXERS = 3
MIX_WIDTH = 1536
XA_HEADS = 4
XA_HEAD_DIM = 128
XA_WIDTH = XA_HEADS * XA_HEAD_DIM
HEAD_DIM = 64
N_Q_HEADS = MIX_WIDTH // HEAD_DIM
N_KV_HEADS = 3
Q_PER_KV = N_Q_HEADS // N_KV_HEADS
WINDOW = 128
ROPE_DIM = HEAD_DIM // 4
ROPE_HALF = ROPE_DIM // 2
ROPE_THETA = 500000.0
SSD_HEAD_DIM = 64
SSD_HEADS = MIX_WIDTH // SSD_HEAD_DIM
SSD_GROUPS = 4
SSD_HPG = SSD_HEADS // SSD_GROUPS
SSD_GROUP_WIDTH = SSD_HPG * SSD_HEAD_DIM
SSD_STATE = 128
SSD_CONV = 4
SSD_CHUNK = 128
SSD_BC = SSD_GROUPS * SSD_STATE
SSD_CONV_CH = MIX_WIDTH + 2 * SSD_BC
CONF_WIDTH = 31
CONF_HALO = 32
SSD_HALO = 8
TOP_K = 4
D_FF = 1024
SWIGLU_LIMIT = 7.0
SWIGLU_ALPHA = 1.702
LN_EPS = 1e-5

MOE_BM = 512
MOE_TF = 512
ROUTER_TM = 512
COMBINE_TM = 256
DISPATCH_TM = 512
DISPATCH_UNROLL = 8
CONF_TR = 256
SWA_TQ = 2 * WINDOW
XA_TQ = 1024
OUT_TM = 512

HI_MASK = 0xFFFF0000

V7X_VMEM_BYTES = 64 * 1024 * 1024
VMEM_LIMIT_BYTES = V7X_VMEM_BYTES * 7 // 8


def _params(n_axes, **kw):
    return pltpu.CompilerParams(dimension_semantics=("arbitrary",) * n_axes,
                                vmem_limit_bytes=VMEM_LIMIT_BYTES, **kw)


def _sigmoid(x):
    return 1.0 / (1.0 + jnp.exp(-x))


def _silu(x):
    return x * _sigmoid(x)


def _pick_tile(n, cap):
    best = LANES
    for t in range(LANES, cap + 1, LANES):
        if n % t == 0:
            best = t
    return best


def _pack_words(a, b):
    a = a.astype(BF16).astype(F32)
    b = b.astype(BF16).astype(F32)
    return (pltpu.bitcast(a, U32) >> 16) | (pltpu.bitcast(b, U32) & jnp.uint32(HI_MASK))


def _unpack_words(w):
    lo = pltpu.bitcast(w << 16, F32)
    hi = pltpu.bitcast(w & jnp.uint32(HI_MASK), F32)
    return lo, hi


def _layer_norm_rows(y, g, b):
    mu = jnp.mean(y, axis=-1, keepdims=True)
    d = y - mu
    var = jnp.mean(d * d, axis=-1, keepdims=True)
    return d * lax.rsqrt(var + LN_EPS) * g + b


def _linear_kernel(x_ref, w_ref, b_ref, o_ref):
    x = x_ref[...].astype(BF16)
    acc = jnp.dot(x, w_ref[...], preferred_element_type=F32) + b_ref[...]
    o_ref[...] = acc.astype(o_ref.dtype)


def _linear(x, w, b, out_dtype, tm=1024, tn_cap=2048):
    m, k = x.shape
    n = w.shape[1]
    tn = _pick_tile(n, tn_cap)
    tm = min(tm, m)
    return pl.pallas_call(
        _linear_kernel,
        grid=(n // tn, m // tm),
        in_specs=[
            pl.BlockSpec((tm, k), lambda j, i: (i, 0)),
            pl.BlockSpec((k, tn), lambda j, i: (0, j)),
            pl.BlockSpec((1, tn), lambda j, i: (0, j)),
        ],
        out_specs=pl.BlockSpec((tm, tn), lambda j, i: (i, j)),
        out_shape=jax.ShapeDtypeStruct((m, n), out_dtype),
        compiler_params=_params(2),
        name="linear",
    )(x, w.astype(BF16), b.reshape(1, n).astype(F32))


def _outproj_ln_kernel(alpha, mix_ref, xa_ref, wm_ref, wx_ref, b_ref, h_ref, g_ref, be_ref,
                       o_ref, obf_ref, opk_ref):
    sub = jnp.dot(mix_ref[...], wm_ref[...], preferred_element_type=F32)
    sub = sub + jnp.dot(xa_ref[...], wx_ref[...], preferred_element_type=F32) + b_ref[...]
    y = _layer_norm_rows(alpha * h_ref[...] + sub, g_ref[...], be_ref[...])
    o_ref[...] = y
    obf_ref[...] = y.astype(BF16)
    tm = y.shape[0]
    for s in range(PACK_SLABS):
        lo = y[:, PACK_SPAN * s:PACK_SPAN * s + LANES]
        hi = y[:, PACK_SPAN * s + LANES:PACK_SPAN * (s + 1)]
        opk_ref[pl.ds(s, tm, stride=PACK_SLABS), :] = _pack_words(lo, hi)


def _outproj_ln(mix, xa, w_out, b_out, h, g, be, alpha):
    t, d = h.shape
    tm = min(OUT_TM, t)
    wm = w_out[:MIX_WIDTH].astype(BF16)
    wx = w_out[MIX_WIDTH:].astype(BF16)
    row = lambda i: (i, 0)
    fixed = lambda i: (0, 0)
    return pl.pallas_call(
        functools.partial(_outproj_ln_kernel, alpha),
        grid=(t // tm,),
        in_specs=[
            pl.BlockSpec((tm, MIX_WIDTH), row),
            pl.BlockSpec((tm, XA_WIDTH), row),
            pl.BlockSpec((MIX_WIDTH, d), fixed),
            pl.BlockSpec((XA_WIDTH, d), fixed),
            pl.BlockSpec((1, d), fixed),
            pl.BlockSpec((tm, d), row),
            pl.BlockSpec((1, d), fixed),
            pl.BlockSpec((1, d), fixed),
        ],
        out_specs=[
            pl.BlockSpec((tm, d), row),
            pl.BlockSpec((tm, d), row),
            pl.BlockSpec((tm * PACK_SLABS, LANES), row),
        ],
        out_shape=[
            jax.ShapeDtypeStruct((t, d), F32),
            jax.ShapeDtypeStruct((t, d), BF16),
            jax.ShapeDtypeStruct((t * PACK_SLABS, LANES), U32),
        ],
        compiler_params=_params(1),
        name="outproj_ln",
    )(mix, xa, wm, wx, b_out.reshape(1, d), h, g.reshape(1, d), be.reshape(1, d))


def _xa_kernel(q_ref, k_ref, v_ref, o_ref):
    scale = XA_HEAD_DIM ** -0.5
    outs = []
    for hd in range(XA_HEADS):
        sl = slice(hd * XA_HEAD_DIM, (hd + 1) * XA_HEAD_DIM)
        s = lax.dot_general(q_ref[:, sl], k_ref[:, sl], (((1,), (1,)), ((), ())),
                            preferred_element_type=F32) * scale
        m = jnp.max(s, axis=-1, keepdims=True)
        p = jnp.exp(s - m)
        den = jnp.sum(p, axis=-1, keepdims=True)
        o = jnp.dot(p.astype(BF16), v_ref[:, sl], preferred_element_type=F32)
        outs.append(o / den)
    o_ref[...] = jnp.concatenate(outs, axis=-1).astype(o_ref.dtype)


def _memory_attention(qm, kv, bsz, seq):
    n_mem = kv.shape[0] // bsz
    tq = min(XA_TQ, seq)
    nq = seq // tq
    return pl.pallas_call(
        _xa_kernel,
        grid=(bsz, nq),
        in_specs=[
            pl.BlockSpec((tq, XA_WIDTH), lambda b, i: (b * nq + i, 0)),
            pl.BlockSpec((n_mem, XA_WIDTH), lambda b, i: (b, 0)),
            pl.BlockSpec((n_mem, XA_WIDTH), lambda b, i: (b, 1)),
        ],
        out_specs=pl.BlockSpec((tq, XA_WIDTH), lambda b, i: (b * nq + i, 0)),
        out_shape=jax.ShapeDtypeStruct((bsz * seq, XA_WIDTH), BF16),
        compiler_params=_params(2),
        name="memory_attention",
    )(qm, kv, kv)


def _rope_tables(pos, freq_row):
    ang = pos * freq_row
    c = jnp.cos(ang)
    s = jnp.sin(ang)
    d = lax.broadcasted_iota(I32, ang.shape, 1) % HEAD_DIM
    cmul = jnp.where(d < ROPE_DIM, c, 1.0)
    s_up = jnp.where(d < ROPE_HALF, -s, 0.0)
    s_dn = jnp.where((d >= ROPE_HALF) & (d < ROPE_DIM), s, 0.0)
    return cmul, s_up, s_dn


def _rope_chunk(t, tabs):
    cmul, s_up, s_dn = tabs
    up = pltpu.roll(t, LANES - ROPE_HALF, axis=1)
    dn = pltpu.roll(t, ROPE_HALF, axis=1)
    return t * cmul + up * s_up + dn * s_dn


def _swa_kernel(sinks_ref, q_ref, kvc_ref, posc_ref, freq_ref, o_ref, kv_win):
    n = pl.program_id(1)
    w = WINDOW
    tq = q_ref.shape[0]
    tab_c = _rope_tables(posc_ref[...], freq_ref[...])
    lane = lax.broadcasted_iota(I32, (tq, LANES), 1)
    first_half = lane < HEAD_DIM

    slot = n % 2

    @pl.when(n == 0)
    def _():
        kv_win[1] = jnp.zeros(kv_win.shape[1:], BF16)

    c0 = _rope_chunk(kvc_ref[:, 0:LANES].astype(F32), tab_c)
    c1 = kvc_ref[:, LANES:2 * LANES].astype(F32)
    c1 = jnp.where(first_half, _rope_chunk(c1, tab_c), c1)
    c2 = kvc_ref[:, 2 * LANES:3 * LANES].astype(F32)
    kv_cur = jnp.concatenate([c0, c1, c2], axis=1).astype(BF16)
    kv_win[slot] = kv_cur[tq - w:, :]
    kv_all = jnp.concatenate([kv_win[1 - slot], kv_cur], axis=0)

    scale = HEAD_DIM ** -0.5
    qs = []
    for j in range(MIX_WIDTH // LANES):
        qj = _rope_chunk(q_ref[:, j * LANES:(j + 1) * LANES].astype(F32), tab_c) * scale
        qs.append(qj.astype(BF16))

    rows = Q_PER_KV * w
    qi = lax.broadcasted_iota(I32, (rows, 2 * w), 0) % w
    sj = lax.broadcasted_iota(I32, (rows, 2 * w), 1)
    rel = qi + w - sj
    band = (rel >= 0) & (rel < w)
    band_first = band & ((n > 0) | (sj >= w))

    blocks = []
    for b in range(tq // w):
        kv = kv_all[b * w:(b + 2) * w, :]
        mask = band_first if b == 0 else band
        outs = []
        for g in range(N_KV_HEADS):
            k_g = kv[:, g * HEAD_DIM:(g + 1) * HEAD_DIM]
            v_g = kv[:, (N_KV_HEADS + g) * HEAD_DIM:(N_KV_HEADS + g + 1) * HEAD_DIM]
            q_parts = []
            sink_parts = []
            for r in range(Q_PER_KV):
                hd = g * Q_PER_KV + r
                chunk = qs[hd // 2]
                q_parts.append(chunk[b * w:(b + 1) * w, (hd % 2) * HEAD_DIM:(hd % 2 + 1) * HEAD_DIM])
                sink_parts.append(jnp.full((w, 1), sinks_ref[hd], F32))
            q_g = jnp.concatenate(q_parts, axis=0)
            sink = jnp.concatenate(sink_parts, axis=0)
            s = lax.dot_general(q_g, k_g, (((1,), (1,)), ((), ())), preferred_element_type=F32)
            s = jnp.where(mask, s, -jnp.inf)
            m = jnp.maximum(jnp.max(s, axis=-1, keepdims=True), sink)
            p = jnp.exp(s - m)
            den = jnp.sum(p, axis=-1, keepdims=True) + jnp.exp(sink - m)
            o = jnp.dot(p.astype(BF16), v_g, preferred_element_type=F32) / den
            for r in range(Q_PER_KV):
                outs.append(o[r * w:(r + 1) * w, :])
        blocks.append(jnp.concatenate(outs, axis=1))
    o_ref[...] = jnp.concatenate(blocks, axis=0).astype(o_ref.dtype)


def _swa_mixer(u, posf, sinks, freq_row, bsz, seq):
    tq = min(SWA_TQ, seq)
    nb = seq // tq
    kv_w = 2 * N_KV_HEADS * HEAD_DIM
    kv_col = MIX_WIDTH // kv_w
    cur = lambda b, n, s: (b * nb + n, 0)
    return pl.pallas_call(
        _swa_kernel,
        grid_spec=pltpu.PrefetchScalarGridSpec(
            num_scalar_prefetch=1,
            grid=(bsz, nb),
            in_specs=[
                pl.BlockSpec((tq, MIX_WIDTH), cur),
                pl.BlockSpec((tq, kv_w), lambda b, n, s: (b * nb + n, kv_col)),
                pl.BlockSpec((tq, 1), cur),
                pl.BlockSpec((1, LANES), lambda b, n, s: (0, 0)),
            ],
            out_specs=pl.BlockSpec((tq, MIX_WIDTH), cur),
            scratch_shapes=[pltpu.VMEM((2, WINDOW, kv_w), BF16)],
        ),
        out_shape=jax.ShapeDtypeStruct((bsz * seq, MIX_WIDTH), BF16),
        compiler_params=_params(2),
        name="swa",
    )(sinks.astype(F32), u, u, posf, freq_row)


def _split3(x):
    p0 = x.astype(BF16)
    r = x - p0.astype(F32)
    p1 = r.astype(BF16)
    p2 = (r - p1.astype(F32)).astype(BF16)
    return p0, p1, p2


def _expand_heads(arr, g):
    rows = arr.shape[0]
    parts = [jnp.broadcast_to(arr[:, g * SSD_HPG + r:g * SSD_HPG + r + 1], (rows, SSD_HEAD_DIM))
             for r in range(SSD_HPG)]
    return jnp.concatenate(parts, axis=1)


def _ssd_kernel(z_ref, xbc_ref, dt_ref, cw_ref, cb_ref, dtb_ref, alog_ref, dskip_ref, ng_ref,
                o_ref, ext_ref, state_ref):
    c = pl.program_id(1)
    q = SSD_CHUNK

    @pl.when(c == 0)
    def _():
        ext_ref[0:SSD_HALO, :] = jnp.zeros((SSD_HALO, SSD_CONV_CH), F32)
        state_ref[...] = jnp.zeros(state_ref.shape, F32)

    ext_ref[SSD_HALO:SSD_HALO + q, :] = xbc_ref[...].astype(F32)
    conv = cb_ref[...]
    for k in range(SSD_CONV):
        conv = conv + cw_ref[k:k + 1, :] * ext_ref[pl.ds(SSD_HALO - (SSD_CONV - 1) + k, q), :]
    ext_ref[0:SSD_HALO, :] = ext_ref[q:q + SSD_HALO, :]
    xbc = _silu(conv)

    lane = lax.broadcasted_iota(I32, (q, LANES), 1)
    x = dt_ref[...] + dtb_ref[...]
    dt = jnp.maximum(x, 0.0) + jnp.log1p(jnp.exp(-jnp.abs(x)))
    a = jnp.where(lane < SSD_HEADS, dt * (-jnp.exp(alog_ref[...])), 0.0)

    row = lax.broadcasted_iota(I32, (q, q), 0)
    col = lax.broadcasted_iota(I32, (q, q), 1)
    causal = row >= col
    tri = causal.astype(BF16)
    a_cs = jnp.zeros((q, LANES), F32)
    for piece in _split3(a):
        a_cs = a_cs + jnp.dot(tri, piece, preferred_element_type=F32)
    a_cs_t = a_cs.T
    a_last = jnp.broadcast_to(a_cs[q - 1:q, :], (q, LANES))

    ys = []
    for g in range(SSD_GROUPS):
        xs_g = xbc[:, g * SSD_GROUP_WIDTH:(g + 1) * SSD_GROUP_WIDTH]
        b_g = xbc[:, MIX_WIDTH + g * SSD_STATE:MIX_WIDTH + (g + 1) * SSD_STATE].astype(BF16)
        c_g = xbc[:, MIX_WIDTH + SSD_BC + g * SSD_STATE:MIX_WIDTH + SSD_BC + (g + 1) * SSD_STATE].astype(BF16)
        cb = lax.dot_general(c_g, b_g, (((1,), (1,)), ((), ())), preferred_element_type=F32)
        acs_e = _expand_heads(a_cs, g)
        last_e = _expand_heads(a_last, g)
        xdt = xs_g * _expand_heads(dt, g)
        xdt_bf = xdt.astype(BF16)

        diag = []
        for r in range(SSD_HPG):
            hd = g * SSD_HPG + r
            seg = a_cs[:, hd:hd + 1] - a_cs_t[hd:hd + 1, :]
            decay = jnp.exp(jnp.where(causal, seg, -jnp.inf))
            mat = (cb * decay).astype(BF16)
            diag.append(jnp.dot(mat, xdt_bf[:, r * SSD_HEAD_DIM:(r + 1) * SSD_HEAD_DIM],
                                preferred_element_type=F32))
        y_diag = jnp.concatenate(diag, axis=1)

        prev = state_ref[g]
        y_off = jnp.dot(c_g, prev.astype(BF16), preferred_element_type=F32) * jnp.exp(acs_e)
        wgt = (xdt * jnp.exp(last_e - acs_e)).astype(BF16)
        upd = lax.dot_general(b_g, wgt, (((0,), (0,)), ((), ())), preferred_element_type=F32)
        state_ref[g] = prev * jnp.exp(last_e[0:1, :]) + upd
        ys.append(y_diag + y_off + dskip_ref[:, g * SSD_GROUP_WIDTH:(g + 1) * SSD_GROUP_WIDTH] * xs_g)

    y = jnp.concatenate(ys, axis=1) * _silu(z_ref[...].astype(F32))
    ms = jnp.mean(y * y, axis=-1, keepdims=True)
    o_ref[...] = (y * lax.rsqrt(ms + LN_EPS) * ng_ref[...]).astype(o_ref.dtype)


def _ssd_mixer(z, xbc, dt_raw, conv_w, conv_b, dt_bias, a_log, d_skip, norm_g, bsz, seq):
    nc = seq // SSD_CHUNK
    pad = LANES - SSD_HEADS
    row = lambda b, c: (b * nc + c, 0)
    fixed = lambda b, c: (0, 0)
    return pl.pallas_call(
        _ssd_kernel,
        grid=(bsz, nc),
        in_specs=[
            pl.BlockSpec((SSD_CHUNK, MIX_WIDTH), row),
            pl.BlockSpec((SSD_CHUNK, SSD_CONV_CH), row),
            pl.BlockSpec((SSD_CHUNK, LANES), row),
            pl.BlockSpec((SSD_CONV, SSD_CONV_CH), fixed),
            pl.BlockSpec((1, SSD_CONV_CH), fixed),
            pl.BlockSpec((1, LANES), fixed),
            pl.BlockSpec((1, LANES), fixed),
            pl.BlockSpec((1, MIX_WIDTH), fixed),
            pl.BlockSpec((1, MIX_WIDTH), fixed),
        ],
        out_specs=pl.BlockSpec((SSD_CHUNK, MIX_WIDTH), row),
        out_shape=jax.ShapeDtypeStruct((bsz * seq, MIX_WIDTH), BF16),
        scratch_shapes=[
            pltpu.VMEM((SSD_HALO + SSD_CHUNK, SSD_CONV_CH), F32),
            pltpu.VMEM((SSD_GROUPS, SSD_STATE, SSD_GROUP_WIDTH), F32),
        ],
        compiler_params=_params(2),
        name="ssd",
    )(z, xbc, dt_raw, conv_w, conv_b.reshape(1, -1),
      jnp.pad(dt_bias, (0, pad)).reshape(1, LANES), jnp.pad(a_log, (0, pad)).reshape(1, LANES),
      jnp.repeat(d_skip, SSD_HEAD_DIM).reshape(1, MIX_WIDTH), norm_g.reshape(1, MIX_WIDTH))


def _conformer_kernel(a_ref, b_ref, w_ref, wb_ref, g_ref, be_ref, o_ref, ext_ref, acc_ref, shift_ref):
    i = pl.program_id(1)
    tr = a_ref.shape[0]

    @pl.when(i == 0)
    def _():
        ext_ref[0:CONF_HALO, :] = jnp.zeros((CONF_HALO, MIX_WIDTH), F32)

    ext_ref[CONF_HALO:CONF_HALO + tr, :] = a_ref[...].astype(F32) * _sigmoid(b_ref[...].astype(F32))
    first = CONF_HALO - (CONF_WIDTH - 1)
    span = shift_ref.shape[1]
    for j in range(MIX_WIDTH // LANES):
        cs = slice(j * LANES, (j + 1) * LANES)
        for a in range(1, SUBLANES):
            shift_ref[a - 1] = ext_ref[pl.ds(a, span), cs]
        acc = jnp.broadcast_to(wb_ref[:, cs], (tr, LANES))
        for k in range(CONF_WIDTH):
            a = (first + k) % SUBLANES
            base = first + k - a
            tap = ext_ref[pl.ds(base, tr), cs] if a == 0 else shift_ref[a - 1, pl.ds(base, tr), :]
            acc = acc + w_ref[k:k + 1, cs] * tap
        acc_ref[:, cs] = acc
    ext_ref[0:CONF_HALO, :] = ext_ref[tr:tr + CONF_HALO, :]
    y = _layer_norm_rows(acc_ref[...], g_ref[...], be_ref[...])
    o_ref[...] = _silu(y).astype(o_ref.dtype)


def _conformer_mixer(u, dw_w, dw_b, ln_g, ln_b, bsz, seq):
    tr = min(CONF_TR, seq)
    nr = seq // tr
    fixed = lambda b, i: (0, 0)
    return pl.pallas_call(
        _conformer_kernel,
        grid=(bsz, nr),
        in_specs=[
            pl.BlockSpec((tr, MIX_WIDTH), lambda b, i: (b * nr + i, 0)),
            pl.BlockSpec((tr, MIX_WIDTH), lambda b, i: (b * nr + i, 1)),
            pl.BlockSpec((CONF_WIDTH, MIX_WIDTH), fixed),
            pl.BlockSpec((1, MIX_WIDTH), fixed),
            pl.BlockSpec((1, MIX_WIDTH), fixed),
            pl.BlockSpec((1, MIX_WIDTH), fixed),
        ],
        out_specs=pl.BlockSpec((tr, MIX_WIDTH), lambda b, i: (b * nr + i, 0)),
        out_shape=jax.ShapeDtypeStruct((bsz * seq, MIX_WIDTH), BF16),
        scratch_shapes=[
            pltpu.VMEM((CONF_HALO + tr, MIX_WIDTH), F32),
            pltpu.VMEM((tr, MIX_WIDTH), F32),
            pltpu.VMEM((SUBLANES - 1, tr + CONF_HALO - SUBLANES, LANES), F32),
        ],
        compiler_params=_params(2),
        name="conformer",
    )(u, u, dw_w, dw_b.reshape(1, -1), ln_g.reshape(1, -1), ln_b.reshape(1, -1))


def _router_kernel(n_experts, h_ref, w_ref, wlo_ref, b_ref, idx_ref, gate_ref, rank_ref, cnt_ref, run_ref):
    i = pl.program_id(0)
    tm = h_ref.shape[0]

    @pl.when(i == 0)
    def _():
        run_ref[...] = jnp.zeros(run_ref.shape, F32)

    x = h_ref[...]
    x_hi = x.astype(BF16)
    x_lo = (x - x_hi.astype(F32)).astype(BF16)
    logits = (jnp.dot(x_hi, w_ref[...], preferred_element_type=F32)
              + jnp.dot(x_lo, w_ref[...], preferred_element_type=F32)
              + jnp.dot(x_hi, wlo_ref[...], preferred_element_type=F32)) + b_ref[...]
    lane = lax.broadcasted_iota(I32, (tm, LANES), 1).astype(F32)
    cur = jnp.where(lane < n_experts, logits, -jnp.inf)
    vals, idxs, hots = [], [], []
    for _ in range(TOP_K):
        m = jnp.max(cur, axis=-1, keepdims=True)
        ix = jnp.min(jnp.where(cur == m, lane, float(LANES)), axis=-1, keepdims=True)
        hot = lane == ix
        vals.append(m)
        idxs.append(ix)
        hots.append(hot)
        cur = jnp.where(hot, -jnp.inf, cur)
    exps = [jnp.exp(v - vals[0]) for v in vals]
    den = exps[0]
    for e in exps[1:]:
        den = den + e

    r = lax.broadcasted_iota(I32, (tm, tm), 0)
    c = lax.broadcasted_iota(I32, (tm, tm), 1)
    before = (r > c).astype(BF16)
    base = run_ref[0:1, :]
    ranks = []
    for hot in hots:
        hot_f = hot.astype(F32)
        earlier = jnp.dot(before, hot.astype(BF16), preferred_element_type=F32)
        ranks.append(jnp.sum(hot_f * (earlier + base), axis=-1, keepdims=True))
        base = base + jnp.sum(hot_f, axis=0, keepdims=True)
    run_ref[0:1, :] = base
    cnt_ref[...] = jnp.broadcast_to(base, cnt_ref.shape)

    idx_out = jnp.zeros((tm, LANES), F32)
    gate_out = jnp.zeros((tm, LANES), F32)
    rank_out = jnp.zeros((tm, LANES), F32)
    for k in range(TOP_K):
        sel = lane == k
        idx_out = jnp.where(sel, idxs[k], idx_out)
        gate_out = jnp.where(sel, exps[k] / den, gate_out)
        rank_out = jnp.where(sel, ranks[k], rank_out)
    idx_ref[...] = idx_out.astype(I32)
    gate_ref[...] = gate_out
    rank_ref[...] = rank_out


def _router(h, router_w, router_b):
    t, d = h.shape
    n_experts = router_w.shape[1]
    tm = min(ROUTER_TM, t)
    w = jnp.pad(router_w.astype(F32), ((0, 0), (0, LANES - n_experts)))
    w_hi = w.astype(BF16)
    w_lo = (w - w_hi.astype(F32)).astype(BF16)
    b = jnp.pad(router_b.astype(F32), (0, LANES - n_experts)).reshape(1, LANES)
    row = lambda i: (i, 0)
    fixed = lambda i: (0, 0)
    return pl.pallas_call(
        functools.partial(_router_kernel, n_experts),
        grid=(t // tm,),
        in_specs=[pl.BlockSpec((tm, d), row), pl.BlockSpec((d, LANES), fixed), pl.BlockSpec((d, LANES), fixed),
                  pl.BlockSpec((1, LANES), fixed)],
        out_specs=[pl.BlockSpec((tm, LANES), row), pl.BlockSpec((tm, LANES), row),
                   pl.BlockSpec((tm, LANES), row), pl.BlockSpec((8, LANES), fixed)],
        out_shape=[jax.ShapeDtypeStruct((t, LANES), I32), jax.ShapeDtypeStruct((t, LANES), F32),
                   jax.ShapeDtypeStruct((t, LANES), F32), jax.ShapeDtypeStruct((8, LANES), F32)],
        scratch_shapes=[pltpu.VMEM((8, LANES), F32)],
        compiler_params=_params(1),
        name="router",
    )(h, w_hi, w_lo, b)


def _dispatch_kernel(tail_ref, pad_ref, used_ref, pos_ref, src_ref, dst_hbm, zero_buf, sem, zero_sem):
    i = pl.program_id(0)
    tm = src_ref.shape[0]
    zr = zero_buf.shape[0]
    n_blocks = dst_hbm.shape[0] // zr

    @pl.when(i == 0)
    def _():
        zero_buf[...] = jnp.zeros(zero_buf.shape, zero_buf.dtype)

        def zero_copy(row, size):
            return pltpu.make_async_copy(zero_buf.at[pl.ds(0, size)], dst_hbm.at[pl.ds(row, size)], zero_sem)

        def padding(act):
            def body(e, carry):
                row, pad = tail_ref[e], pad_ref[e]
                size = zr // 2
                while size >= 1:
                    piece = pad & size

                    @pl.when(piece != 0)
                    def _(row=row, size=size):
                        act(zero_copy(row, size))

                    row = row + piece
                    size //= 2
                return carry
            return body

        def blocks(act):
            def body(b, carry):
                act(zero_copy(b * zr, zr))
                return carry
            return body

        n_e = tail_ref.shape[0]
        lax.fori_loop(0, n_e, padding(lambda c: c.start()), 0)
        lax.fori_loop(used_ref[0], n_blocks, blocks(lambda c: c.start()), 0)
        lax.fori_loop(0, n_e, padding(lambda c: c.wait()), 0)
        lax.fori_loop(used_ref[0], n_blocks, blocks(lambda c: c.wait()), 0)

    def issue(jo, carry):
        for ji in range(DISPATCH_UNROLL):
            j = jo * DISPATCH_UNROLL + ji
            for k in range(TOP_K):
                pltpu.make_async_copy(src_ref.at[j], dst_hbm.at[pos_ref[j * TOP_K + k]],
                                      sem).start(priority=k % 2)
        return carry

    lax.fori_loop(0, tm // DISPATCH_UNROLL, issue, 0)
    n = tm * TOP_K
    pltpu.make_async_copy(dst_hbm.at[pl.ds(0, n)], dst_hbm.at[pl.ds(0, n)], sem).wait()


def _dispatch(h_packed, pos_flat, tail_row, pad_len, n_used, n_rows):
    t = h_packed.shape[0]
    tm = min(DISPATCH_TM, t)
    return pl.pallas_call(
        _dispatch_kernel,
        grid_spec=pltpu.PrefetchScalarGridSpec(
            num_scalar_prefetch=3,
            grid=(t // tm,),
            in_specs=[
                pl.BlockSpec((tm * TOP_K,), lambda i, tr, pd, nu: (i,), memory_space=pltpu.SMEM),
                pl.BlockSpec((tm, PACK_SLABS, LANES), lambda i, tr, pd, nu: (i, 0, 0)),
            ],
            out_specs=pl.BlockSpec(memory_space=pl.ANY),
            scratch_shapes=[pltpu.VMEM((MOE_BM, PACK_SLABS, LANES), U32),
                            pltpu.SemaphoreType.DMA(()), pltpu.SemaphoreType.DMA(())],
        ),
        out_shape=jax.ShapeDtypeStruct((n_rows, PACK_SLABS, LANES), U32),
        compiler_params=_params(1, has_side_effects=True),
        name="dispatch",
    )(tail_row, pad_len, n_used, pos_flat, h_packed)


def _expert_changed(be_ref, i):
    return (i == 0) | (be_ref[i] != be_ref[jnp.maximum(i - 1, 0)])


def _moe_up_kernel(be_ref, nu_ref, x_ref, wg_ref, wl_ref, bg_ref, bl_ref, o_ref, wg_bf, wl_bf, x_bf):
    i = pl.program_id(1)
    bm = o_ref.shape[0]

    @pl.when(_expert_changed(be_ref, i))
    def _():
        wg_bf[...] = wg_ref[0].astype(BF16)
        wl_bf[...] = wl_ref[0].astype(BF16)

    @pl.when(i < nu_ref[0])
    def _():
        for s in range(PACK_SLABS):
            lo, hi = _unpack_words(x_ref[pl.ds(s, bm, stride=PACK_SLABS), :])
            x_bf[:, PACK_SPAN * s:PACK_SPAN * s + LANES] = lo.astype(BF16)
            x_bf[:, PACK_SPAN * s + LANES:PACK_SPAN * (s + 1)] = hi.astype(BF16)
        x = x_bf[...]
        glu = jnp.dot(x, wg_bf[...], preferred_element_type=F32) + bg_ref[0]
        lin = jnp.dot(x, wl_bf[...], preferred_element_type=F32) + bl_ref[0]
        glu = jnp.minimum(glu, SWIGLU_LIMIT)
        lin = jnp.clip(lin, -SWIGLU_LIMIT, SWIGLU_LIMIT)
        o_ref[...] = (glu * _sigmoid(SWIGLU_ALPHA * glu) * (lin + 1.0)).astype(o_ref.dtype)

    @pl.when(i >= nu_ref[0])
    def _():
        o_ref[...] = jnp.zeros(o_ref.shape, o_ref.dtype)


def _moe_up(block_e, n_used, xs_flat, w1, b1, layer, n_blocks):
    n_layers, n_e, d, two_ff = w1.shape
    d_ff = two_ff // 2
    tf = MOE_TF
    nt = d_ff // tf
    bm = MOE_BM
    b1r = b1.reshape(n_layers, n_e, 1, two_ff)
    return pl.pallas_call(
        _moe_up_kernel,
        grid_spec=pltpu.PrefetchScalarGridSpec(
            num_scalar_prefetch=2,
            grid=(nt, n_blocks),
            in_specs=[
                pl.BlockSpec((bm * PACK_SLABS, LANES), lambda n, i, be, nu: (i, 0)),
                pl.BlockSpec((None, 1, d, tf), lambda n, i, be, nu: (layer, be[i], 0, n)),
                pl.BlockSpec((None, 1, d, tf), lambda n, i, be, nu: (layer, be[i], 0, n + nt)),
                pl.BlockSpec((None, 1, 1, tf), lambda n, i, be, nu: (layer, be[i], 0, n)),
                pl.BlockSpec((None, 1, 1, tf), lambda n, i, be, nu: (layer, be[i], 0, n + nt)),
            ],
            out_specs=pl.BlockSpec((bm, tf), lambda n, i, be, nu: (i, n)),
            scratch_shapes=[pltpu.VMEM((d, tf), BF16), pltpu.VMEM((d, tf), BF16), pltpu.VMEM((bm, d), BF16)],
        ),
        out_shape=jax.ShapeDtypeStruct((n_blocks * bm, d_ff), BF16),
        compiler_params=_params(2),
        name="moe_up",
    )(block_e, n_used, xs_flat, w1, w1, b1r, b1r)


def _moe_down_kernel(be_ref, nu_ref, a_ref, w_ref, b_ref, o_ref, w_bf):
    i = pl.program_id(0)
    bm = a_ref.shape[0]

    @pl.when(_expert_changed(be_ref, i))
    def _():
        w_bf[...] = w_ref[0].astype(BF16)

    @pl.when(i < nu_ref[0])
    def _():
        y = jnp.dot(a_ref[...], w_bf[...], preferred_element_type=F32) + b_ref[0]
        for s in range(PACK_SLABS):
            lo = y[:, PACK_SPAN * s:PACK_SPAN * s + LANES]
            hi = y[:, PACK_SPAN * s + LANES:PACK_SPAN * (s + 1)]
            o_ref[pl.ds(s, bm, stride=PACK_SLABS), :] = _pack_words(lo, hi)

    @pl.when(i >= nu_ref[0])
    def _():
        o_ref[...] = jnp.zeros(o_ref.shape, o_ref.dtype)


def _moe_down(block_e, n_used, act, w2, b2, layer, n_blocks):
    n_layers, n_e, d_ff, d = w2.shape
    bm = MOE_BM
    return pl.pallas_call(
        _moe_down_kernel,
        grid_spec=pltpu.PrefetchScalarGridSpec(
            num_scalar_prefetch=2,
            grid=(n_blocks,),
            in_specs=[
                pl.BlockSpec((bm, d_ff), lambda i, be, nu: (i, 0)),
                pl.BlockSpec((None, 1, d_ff, d), lambda i, be, nu: (layer, be[i], 0, 0)),
                pl.BlockSpec((None, 1, 1, d), lambda i, be, nu: (layer, be[i], 0, 0)),
            ],
            out_specs=pl.BlockSpec((bm * PACK_SLABS, LANES), lambda i, be, nu: (i, 0)),
            scratch_shapes=[pltpu.VMEM((d_ff, d), BF16)],
        ),
        out_shape=jax.ShapeDtypeStruct((n_blocks * bm * PACK_SLABS, LANES), U32),
        compiler_params=_params(1),
        name="moe_down",
    )(block_e, n_used, act, w2, b2.reshape(n_layers, n_e, 1, d))


def _combine_ln_kernel(alpha, pos_ref, pos_next_ref, rows_hbm, rows_flat_hbm, gate_ref, h_ref, g_ref,
                       be_ref, o_ref, obf_ref, buf_a, buf_b, sem):
    i = pl.program_id(0)
    last = pl.num_programs(0) - 1
    tm = h_ref.shape[0]
    n = tm * TOP_K * PACK_SLABS

    def issue(p_ref, buf, sem_k, lo, hi):
        for j in range(lo, hi):
            for k in range(TOP_K):
                pltpu.make_async_copy(rows_hbm.at[p_ref[j * TOP_K + k]],
                                      buf.at[pl.ds((k * tm + j) * PACK_SLABS, PACK_SLABS), :],
                                      sem_k).start(priority=k % 2)

    def wait(buf, sem_k):
        pltpu.make_async_copy(rows_flat_hbm.at[pl.ds(0, n)], buf, sem_k).wait()

    def step(cur, nxt, cur_sem, nxt_sem, is_even):
        if is_even:
            @pl.when(i == 0)
            def _():
                issue(pos_ref, cur, cur_sem, 0, tm)

        wait(cur, cur_sem)
        gates = [gate_ref[:, k:k + 1] for k in range(TOP_K)]
        cols = []
        per = tm // PACK_SLABS
        for s in range(PACK_SLABS):
            lo_acc = jnp.zeros((tm, LANES), F32)
            hi_acc = jnp.zeros((tm, LANES), F32)
            for k in range(TOP_K):
                lo, hi = _unpack_words(cur[pl.ds(k * tm * PACK_SLABS + s, tm, stride=PACK_SLABS), :])
                lo_acc = lo_acc + gates[k] * lo
                hi_acc = hi_acc + gates[k] * hi
            cols.append(lo_acc)
            cols.append(hi_acc)
            issue(pos_next_ref, nxt, nxt_sem, s * per, (s + 1) * per)
        ffn = jnp.concatenate(cols, axis=1)
        y = _layer_norm_rows(alpha * h_ref[...] + ffn, g_ref[...], be_ref[...])
        o_ref[...] = y
        obf_ref[...] = y.astype(BF16)

        @pl.when(i == last)
        def _():
            wait(nxt, nxt_sem)

    @pl.when(i % 2 == 0)
    def _():
        step(buf_a, buf_b, sem.at[0], sem.at[1], True)

    @pl.when(i % 2 == 1)
    def _():
        step(buf_b, buf_a, sem.at[1], sem.at[0], False)


def _combine_ln(pos_flat, rows, gates, h, g, be, alpha):
    t, d = h.shape
    tm = min(COMBINE_TM, t)
    last = t // tm - 1
    row = lambda i: (i, 0)
    fixed = lambda i: (0, 0)
    return pl.pallas_call(
        functools.partial(_combine_ln_kernel, alpha),
        grid=(t // tm,),
        in_specs=[
            pl.BlockSpec((tm * TOP_K,), lambda i: (i,), memory_space=pltpu.SMEM),
            pl.BlockSpec((tm * TOP_K,), lambda i: (jnp.minimum(i + 1, last),), memory_space=pltpu.SMEM),
            pl.BlockSpec(memory_space=pl.ANY),
            pl.BlockSpec(memory_space=pl.ANY),
            pl.BlockSpec((tm, LANES), row),
            pl.BlockSpec((tm, d), row),
            pl.BlockSpec((1, d), fixed),
            pl.BlockSpec((1, d), fixed),
        ],
        out_specs=[pl.BlockSpec((tm, d), row), pl.BlockSpec((tm, d), row)],
        out_shape=[jax.ShapeDtypeStruct((t, d), F32), jax.ShapeDtypeStruct((t, d), BF16)],
        scratch_shapes=[pltpu.VMEM((tm * TOP_K * PACK_SLABS, LANES), U32),
                        pltpu.VMEM((tm * TOP_K * PACK_SLABS, LANES), U32), pltpu.SemaphoreType.DMA((2,))],
        compiler_params=_params(1),
        name="combine_ln",
    )(pos_flat, pos_flat, rows, rows.reshape(-1, LANES), gates, h, g.reshape(1, d), be.reshape(1, d))


def _moe_layer(h, h_packed, router_w, router_b, w1, b1, w2, b2, layer, g, be, alpha):
    t, _ = h.shape
    n_e = router_w.shape[1]
    bm = MOE_BM
    n_blocks = -(-(t * TOP_K) // bm) + n_e
    idx_o, gate_o, rank_o, cnt_o = _router(h, router_w, router_b)
    idx = idx_o[:, :TOP_K]
    counts = cnt_o[0, :n_e].astype(I32)
    pcounts = (counts + bm - 1) // bm * bm
    pend = jnp.cumsum(pcounts)
    pstart = pend - pcounts
    hot = idx[:, :, None] == jnp.arange(n_e, dtype=I32)
    pos = (jnp.sum(jnp.where(hot, pstart, 0), axis=-1) + rank_o[:, :TOP_K].astype(I32)).reshape(-1)
    blk_row = jnp.arange(n_blocks, dtype=I32) * bm
    block_e = jnp.minimum(jnp.sum((pend[None, :] <= blk_row[:, None]).astype(I32), axis=1), n_e - 1)
    n_used = (pend[-1:] // bm).astype(I32)
    tail_row = (pstart + counts).astype(I32)
    pad_len = (pcounts - counts).astype(I32)
    xs = _dispatch(h_packed.reshape(t, PACK_SLABS, LANES), pos, tail_row, pad_len, n_used, n_blocks * bm)
    act = _moe_up(block_e, n_used, xs.reshape(n_blocks * bm * PACK_SLABS, LANES), w1, b1, layer, n_blocks)
    rows = _moe_down(block_e, n_used, act, w2, b2, layer, n_blocks)
    return _combine_ln(pos, rows.reshape(n_blocks * bm, PACK_SLABS, LANES), gate_o, h, g, be, alpha)


def kernel(x, mem, positions, attn_w_in, attn_b_in, attn_sinks, ssd_w_in, ssd_b_in, ssd_conv_w, ssd_conv_b, ssd_dt_bias, ssd_a_log, ssd_d_skip, ssd_norm_g, conf_w_in, conf_b_in, conf_dw_w, conf_dw_b, conf_ln_g, conf_ln_b, mem_w_kv, w_out, b_out, ln1_g, ln1_b, router_w, router_b, moe_w1, moe_b1, moe_w2, moe_b2, ln2_g, ln2_b):
    bsz, seq, d = x.shape
    depth = w_out.shape[0]
    t = bsz * seq
    alpha = (2 * depth) ** 0.25
    n_mem = mem.shape[1]

    posf = positions.astype(F32).reshape(t, 1)
    inv_freq = ROPE_THETA ** (-jnp.arange(ROPE_HALF, dtype=F32) / ROPE_HALF)
    lane = jnp.arange(LANES) % HEAD_DIM
    freq_row = jnp.where(lane < ROPE_DIM, inv_freq[lane % ROPE_HALF], 0.0).reshape(1, LANES)
    mem2 = mem.reshape(bsz * n_mem, d)

    h = x.reshape(t, d)
    h_in = h
    for i in range(depth):
        kind, j = i % N_MIXERS, i // N_MIXERS
        if kind == 0:
            w_in, b_in = attn_w_in[j], attn_b_in[j]
            u = _linear(h_in, w_in[:, :-XA_WIDTH], b_in[:-XA_WIDTH], BF16)
            mix = _swa_mixer(u, posf, attn_sinks[j], freq_row, bsz, seq)
        elif kind == 1:
            w_in, b_in = ssd_w_in[j], ssd_b_in[j]
            z = _linear(h_in, w_in[:, :MIX_WIDTH], b_in[:MIX_WIDTH], BF16)
            xbc = _linear(h_in, w_in[:, MIX_WIDTH:MIX_WIDTH + SSD_CONV_CH],
                          b_in[MIX_WIDTH:MIX_WIDTH + SSD_CONV_CH], F32)
            dt_lo = MIX_WIDTH + SSD_CONV_CH
            pad = LANES - SSD_HEADS
            dt_raw = _linear(h_in, jnp.pad(w_in[:, dt_lo:dt_lo + SSD_HEADS], ((0, 0), (0, pad))),
                             jnp.pad(b_in[dt_lo:dt_lo + SSD_HEADS], (0, pad)), F32)
            mix = _ssd_mixer(z, xbc, dt_raw, ssd_conv_w[j], ssd_conv_b[j], ssd_dt_bias[j], ssd_a_log[j],
                             ssd_d_skip[j], ssd_norm_g[j], bsz, seq)
        else:
            w_in, b_in = conf_w_in[j], conf_b_in[j]
            u = _linear(h_in, w_in[:, :-XA_WIDTH], b_in[:-XA_WIDTH], BF16)
            mix = _conformer_mixer(u, conf_dw_w[j], conf_dw_b[j], conf_ln_g[j], conf_ln_b[j], bsz, seq)
        qm = _linear(h_in, w_in[:, -XA_WIDTH:], b_in[-XA_WIDTH:], BF16)
        kv = _linear(mem2, mem_w_kv[i], jnp.zeros((2 * XA_WIDTH,), F32), BF16)
        xa = _memory_attention(qm, kv, bsz, seq)
        h1, _, h1_packed = _outproj_ln(mix, xa, w_out[i], b_out[i], h, ln1_g[i], ln1_b[i], alpha)
        h, h_in = _moe_layer(h1, h1_packed, router_w[i], router_b[i], moe_w1, moe_b1, moe_w2, moe_b2, i,
                             ln2_g[i], ln2_b[i], alpha)
    return h.reshape(bsz, seq, d)
```
